```python
import jax, jax.numpy as jnp
from jax import lax
import numpy as np

D_MODEL = 2048
BATCH = 1
SEQ = 16384
DEPTH = 2
DEC_BATCH = 8
DEC_SEQ = 32
PAST_LEN = 2048

CHUNK = 64
NORM_EPS = 1e-6
D_POOL = D_MODEL // 2
POOL_WINDOWS = (2, 4, 8, 16)
N_POOL_GROUPS = len(POOL_WINDOWS)
POOL_GC = D_POOL // N_POOL_GROUPS
POOL_HIST = max(POOL_WINDOWS) - 1
D_RWKV = D_MODEL - D_POOL
RWKV_HEAD = 64
RWKV_HEADS = D_RWKV // RWKV_HEAD
DECAY_LORA = 64
ICLR_LORA = 64
GATE_LORA = 128
D_SHIFT = 3 * D_RWKV + DECAY_LORA + ICLR_LORA + GATE_LORA
D_IN0 = D_POOL + D_SHIFT
SHIFT_SPLITS = (D_RWKV, 2 * D_RWKV, 3 * D_RWKV, 3 * D_RWKV + DECAY_LORA, 3 * D_RWKV + DECAY_LORA + ICLR_LORA)
LNX_EPS = 64e-5
ATT_HEADS = 16
ATT_HEAD_DIM = D_MODEL // ATT_HEADS
N_PREV_CHUNKS = 8
PREV_ROWS = N_PREV_CHUNKS * CHUNK
BAND = PREV_ROWS + CHUNK
REL_CLIP = 2 * CHUNK
D_FF = 4 * D_MODEL

kernel_name = 'chunk_stream_pool_rwkv7_bandattn'


def rmsnorm(x, g):
    xf = x.astype(jnp.float32)
    y = xf * lax.rsqrt(jnp.mean(xf * xf, axis=-1, keepdims=True) + NORM_EPS)
    return (y * g.astype(jnp.float32)).astype(x.dtype)


def pool_mix(u, hist, pos0, w_pool, pool_scale):
    T = u.shape[1]
    full = jnp.concatenate([hist.astype(u.dtype), u], axis=1)
    cs = jnp.pad(jnp.cumsum(full.astype(jnp.float32), axis=1), ((0, 0), (1, 0), (0, 0)))
    pos = (pos0 + jnp.arange(T)).astype(jnp.float32)
    end = POOL_HIST + 1
    outs = []
    for gi, win in enumerate(POOL_WINDOWS):
        sl = slice(gi * POOL_GC, (gi + 1) * POOL_GC)
        wsum = cs[:, end:end + T, sl] - cs[:, end - win:end - win + T, sl]
        cnt = jnp.minimum(jnp.float32(win), pos + 1.0)
        d = wsum / cnt[None, :, None] - u[..., sl].astype(jnp.float32)
        outs.append(jnp.einsum('btc,cd->btd', d.astype(u.dtype), w_pool[gi]))
    out = jnp.concatenate(outs, axis=-1) * pool_scale
    return out, full[:, -POOL_HIST:]


def wkv7_scan(S0, r, decay, k, v, a, b):
    def step(S, xs):
        r_t, w_t, k_t, v_t, a_t, b_t = xs
        sa = jnp.einsum('bhvk,bhk->bhv', S, a_t)
        S = S * w_t[:, :, None, :] + sa[..., None] * b_t[:, :, None, :] + v_t[..., None] * k_t[:, :, None, :]
        return S, jnp.einsum('bhvk,bhk->bhv', S, r_t)
    xs = tuple(jnp.moveaxis(t.astype(jnp.float32), 1, 0) for t in (r, decay, k, v, a, b))
    S, ys = lax.scan(step, S0.astype(jnp.float32), xs)
    return jnp.moveaxis(ys, 0, 1), S


def pool_rwkv_mixer(h, pool_hist, shift_prev, wkv0, pos0, P):
    B, T, _ = h.shape
    f32 = jnp.float32
    heads = lambda t: t.reshape(B, T, RWKV_HEADS, RWKV_HEAD)
    p = jnp.einsum('btd,de->bte', h, P['w_in0'])
    u, z = p[..., :D_POOL], p[..., D_POOL:]
    pool_out, pool_new = pool_mix(u, pool_hist, pos0, P['w_pool'], P['pool_scale'])
    z_prev = jnp.concatenate([shift_prev[:, None, :].astype(z.dtype), z[:, :-1]], axis=1)
    zs = z + (z_prev - z) * P['mu_shift']
    r, k, v, xw, xa, xg = jnp.split(zs, SHIFT_SPLITS, axis=-1)
    w_log = -jax.nn.softplus(-(P['w0_decay'] + jnp.tanh(xw) @ P['w2_decay']).astype(f32)) - 0.5
    decay = jnp.exp(-jnp.exp(w_log))
    a = jax.nn.sigmoid((P['a0_iclr'] + xa @ P['a2_iclr']).astype(f32))
    g = jax.nn.sigmoid(xg) @ P['g2_gate']
    kk = heads((k * P['k_k']).astype(f32))
    kk = kk / jnp.maximum(jnp.sqrt(jnp.sum(kk * kk, axis=-1, keepdims=True)), 1e-12)
    k_mod = k.astype(f32) * (1.0 + (a - 1.0) * P['k_a'].astype(f32))
    r_h, k_h, v_h = heads(r.astype(f32)), heads(k_mod), heads(v.astype(f32))
    y, S = wkv7_scan(wkv0, r_h, heads(decay), k_h, v_h, -kk, kk * heads(a))
    mu = jnp.mean(y, axis=-1, keepdims=True)
    var = jnp.mean(jnp.square(y - mu), axis=-1, keepdims=True)
    yn = ((y - mu) * lax.rsqrt(var + LNX_EPS)).reshape(B, T, D_RWKV) * P['lnx_w'] + P['lnx_b']
    bonus = jnp.sum(r_h * k_h * P['r_k'].astype(f32), axis=-1, keepdims=True) * v_h
    y = (yn + bonus.reshape(B, T, D_RWKV)) * g.astype(f32)
    mixed = jnp.concatenate([pool_out, y.astype(h.dtype)], axis=-1)
    return mixed @ P['w_out0'], (pool_new, z[:, -1], S)


def band_attend(q, k, v, q_pos, k_pos, rel_bias):
    s = jnp.einsum('bqhd,bkhd->bhqk', q, k).astype(jnp.float32) * (ATT_HEAD_DIM ** -0.5)
    rel = jnp.clip(q_pos[:, None] - k_pos[None, :], -REL_CLIP, REL_CLIP) + REL_CLIP
    bias = jnp.take(rel_bias.astype(jnp.float32), rel, axis=1)
    qc = q_pos // CHUNK
    kc = k_pos // CHUNK
    ok = (k_pos[None, :] >= 0) & (kc[None, :] <= qc[:, None]) & (kc[None, :] >= qc[:, None] - N_PREV_CHUNKS)
    s = jnp.where(ok, s + bias, -1e30)
    p = jax.nn.softmax(s, axis=-1)
    return jnp.einsum('bhqk,bkhd->bqhd', p.astype(v.dtype), v)


def band_attn_prompt(q, k, v, rel_bias):
    B, T = q.shape[0], q.shape[1]
    n_chunks = T // CHUNK
    pad = ((0, 0), (PREV_ROWS, 0), (0, 0), (0, 0))
    k_pad, v_pad = jnp.pad(k, pad), jnp.pad(v, pad)

    def one_chunk(ci):
        start = ci * CHUNK
        qb = lax.dynamic_slice_in_dim(q, start, CHUNK, axis=1)
        kb = lax.dynamic_slice_in_dim(k_pad, start, BAND, axis=1)
        vb = lax.dynamic_slice_in_dim(v_pad, start, BAND, axis=1)
        q_pos = start + jnp.arange(CHUNK)
        k_pos = start - PREV_ROWS + jnp.arange(BAND)
        return band_attend(qb, kb, vb, q_pos, k_pos, rel_bias)

    o = lax.map(one_chunk, jnp.arange(n_chunks))
    return jnp.moveaxis(o, 0, 1).reshape(B, T, ATT_HEADS, ATT_HEAD_DIM)


def band_attn_sample(q, k, v, cache_k, cache_v, rel_bias):
    T, R = q.shape[1], cache_k.shape[1]
    kb = jnp.concatenate([cache_k.astype(k.dtype), k], axis=1)
    vb = jnp.concatenate([cache_v.astype(v.dtype), v], axis=1)
    k_pos = PAST_LEN - R + jnp.arange(R + T)
    q_pos = PAST_LEN + jnp.arange(T)
    return band_attend(q, kb, vb, q_pos, k_pos, rel_bias)


def band_attn_mixer(h, cache_k, cache_v, P):
    B, T, _ = h.shape
    qkv = (h @ P['w_qkv1']).reshape(B, T, 3, ATT_HEADS, ATT_HEAD_DIM)
    q, k, v = qkv[:, :, 0], qkv[:, :, 1], qkv[:, :, 2]
    if cache_k is None:
        o = band_attn_prompt(q, k, v, P['rel_bias'])
        keep = min(PREV_ROWS, T)
        new = (k[:, T - keep:], v[:, T - keep:])
    else:
        o = band_attn_sample(q, k, v, cache_k, cache_v, P['rel_bias'])
        new = (k, v)
    return o.reshape(B, T, D_MODEL) @ P['w_out1'], new


def run_group(x, c, pool_hist, shift_prev, wkv0, cache_k, cache_v, pos0, P):
    for layer in range(DEPTH):
        mod = jnp.einsum('bd,de->be', jax.nn.silu(c), P['w_ada'][layer]) + P['b_ada'][layer]
        sh1, sc1, gt1, sh2, sc2, gt2 = jnp.split(mod[:, None, :], 6, axis=-1)
        gn = P['g_norm'][layer]
        h = rmsnorm(x, gn[0]) * (1.0 + sc1) + sh1
        if layer % 2 == 0:
            o, (pool_new, shift_new, wkv_new) = pool_rwkv_mixer(h, pool_hist, shift_prev, wkv0, pos0, P)
        else:
            o, (k_new, v_new) = band_attn_mixer(h, cache_k, cache_v, P)
        x = x + gt1 * rmsnorm(o, gn[1])
        h = rmsnorm(x, gn[2]) * (1.0 + sc2) + sh2
        f = jnp.square(jax.nn.relu(h @ P['w_ff1'][layer])) @ P['w_ff2'][layer]
        x = x + gt2 * rmsnorm(f, gn[3])
    return x, pool_new, shift_new, wkv_new, k_new, v_new


def setup_inputs(seed: int = 0) -> dict:
    key = jax.random.key(seed)
    ks = iter(jax.random.split(key, 48))
    f32 = jnp.float32
    nrm = lambda shape, scale: jax.random.normal(next(ks), shape, f32) * scale
    D = D_MODEL
    att_rows = min(PREV_ROWS, PAST_LEN)
    return {
        'x_prompt': nrm((BATCH, SEQ, D), 1.0),
        'x_sample': nrm((DEC_BATCH, DEC_SEQ, D), 1.0),
        'c_prompt': nrm((BATCH, D), 1.0),
        'c_sample': nrm((DEC_BATCH, D), 1.0),
        'state_l0_pool': nrm((DEC_BATCH, POOL_HIST, D_POOL), 1.0),
        'state_l0_shift': nrm((DEC_BATCH, D_SHIFT), 1.0),
        'state_l0_wkv': nrm((DEC_BATCH, RWKV_HEADS, RWKV_HEAD, RWKV_HEAD), 0.5),
        'cache_l1_k': nrm((DEC_BATCH, att_rows, ATT_HEADS, ATT_HEAD_DIM), 1.0),
        'cache_l1_v': nrm((DEC_BATCH, att_rows, ATT_HEADS, ATT_HEAD_DIM), 1.0),
        'w_ada': nrm((DEPTH, D, 6 * D), 0.5 * D ** -0.5),
        'b_ada': nrm((DEPTH, 6 * D), 0.02),
        'g_norm': 1.0 + nrm((DEPTH, 4, D), 0.05),
        'w_in0': nrm((D, D_IN0), D ** -0.5),
        'w_pool': nrm((N_POOL_GROUPS, POOL_GC, POOL_GC), POOL_GC ** -0.5),
        'pool_scale': 1.0 + nrm((D_POOL,), 0.1),
        'mu_shift': jax.random.uniform(next(ks), (D_SHIFT,), f32),
        'w0_decay': jax.random.uniform(next(ks), (D_RWKV,), f32, -6.0, -1.0),
        'w2_decay': nrm((DECAY_LORA, D_RWKV), 0.5 * DECAY_LORA ** -0.5),
        'a0_iclr': nrm((D_RWKV,), 0.5),
        'a2_iclr': nrm((ICLR_LORA, D_RWKV), ICLR_LORA ** -0.5),
        'g2_gate': nrm((GATE_LORA, D_RWKV), GATE_LORA ** -0.5),
        'k_k': 0.85 + nrm((D_RWKV,), 0.05),
        'k_a': 1.0 + nrm((D_RWKV,), 0.05),
        'r_k': nrm((RWKV_HEADS, RWKV_HEAD), 0.1),
        'lnx_w': 1.0 + nrm((D_RWKV,), 0.05),
        'lnx_b': nrm((D_RWKV,), 0.02),
        'w_out0': nrm((D, D), D ** -0.5),
        'w_qkv1': nrm((D, 3 * D), D ** -0.5),
        'rel_bias': nrm((ATT_HEADS, 2 * REL_CLIP + 1), 0.5),
        'w_out1': nrm((D, D), D ** -0.5),
        'w_ff1': nrm((DEPTH, D, D_FF), D ** -0.5),
        'w_ff2': nrm((DEPTH, D_FF, D), D_FF ** -0.5),
    }


def reference(x_prompt, x_sample, c_prompt, c_sample, state_l0_pool, state_l0_shift, state_l0_wkv,
              cache_l1_k, cache_l1_v, w_ada, b_ada, g_norm, w_in0, w_pool, pool_scale, mu_shift,
              w0_decay, w2_decay, a0_iclr, a2_iclr, g2_gate, k_k, k_a, r_k, lnx_w, lnx_b, w_out0,
              w_qkv1, rel_bias, w_out1, w_ff1, w_ff2):
    P = {'w_ada': w_ada, 'b_ada': b_ada, 'g_norm': g_norm, 'w_in0': w_in0, 'w_pool': w_pool,
         'pool_scale': pool_scale, 'mu_shift': mu_shift, 'w0_decay': w0_decay, 'w2_decay': w2_decay,
         'a0_iclr': a0_iclr, 'a2_iclr': a2_iclr, 'g2_gate': g2_gate, 'k_k': k_k, 'k_a': k_a,
         'r_k': r_k, 'lnx_w': lnx_w, 'lnx_b': lnx_b, 'w_out0': w_out0, 'w_qkv1': w_qkv1,
         'rel_bias': rel_bias, 'w_out1': w_out1, 'w_ff1': w_ff1, 'w_ff2': w_ff2}
    B = x_prompt.shape[0]
    y_prompt, pool_p, shift_p, wkv_p, k_p, v_p = run_group(
        x_prompt, c_prompt,
        jnp.zeros((B, POOL_HIST, D_POOL), x_prompt.dtype),
        jnp.zeros((B, D_SHIFT), x_prompt.dtype),
        jnp.zeros((B, RWKV_HEADS, RWKV_HEAD, RWKV_HEAD), jnp.float32),
        None, None, 0, P)
    y_sample, pool_s, shift_s, wkv_s, k_s, v_s = run_group(
        x_sample, c_sample, state_l0_pool, state_l0_shift, state_l0_wkv,
        cache_l1_k, cache_l1_v, PAST_LEN, P)
    return (y_prompt, y_sample,
            pool_p.astype(state_l0_pool.dtype), pool_s.astype(state_l0_pool.dtype),
            shift_p.astype(state_l0_shift.dtype), shift_s.astype(state_l0_shift.dtype),
            wkv_p.astype(state_l0_wkv.dtype), wkv_s.astype(state_l0_wkv.dtype),
            k_p.astype(cache_l1_k.dtype), v_p.astype(cache_l1_v.dtype),
            k_s.astype(cache_l1_k.dtype), v_s.astype(cache_l1_v.dtype))
```

```python
import functools

import jax
import jax.numpy as jnp
from jax import lax
from jax.experimental import pallas as pl
from jax.experimental.pallas import tpu as pltpu

F32 = jnp.float32
BF16 = jnp.bfloat16

NORM_EPS = 1e-6
LNX_EPS = 64e-5
CHUNK = 64
POOL_WINDOWS = (2, 4, 8, 16)
POOL_PAD = 16
RWKV_HEAD = 64
N_PREV_CHUNKS = 8
PREV_ROWS = N_PREV_CHUNKS * CHUNK
REL_CLIP = 2 * CHUNK
PAST_LEN = 2048
NEG_INF = -1e30
V7X_VMEM_BYTES = 64 * 1024 * 1024
LANES = 128


def _cparams(semantics, vmem_mb):
    assert vmem_mb * 1024 * 1024 <= V7X_VMEM_BYTES
    return pltpu.CompilerParams(dimension_semantics=semantics, vmem_limit_bytes=vmem_mb * 1024 * 1024)


def _dot(a, b):
    return jnp.dot(a.astype(BF16), b.astype(BF16), preferred_element_type=F32)


def _dot_nt(a, b):
    return lax.dot_general(a.astype(BF16), b.astype(BF16), (((1,), (1,)), ((), ())), preferred_element_type=F32)


def _split3(x):
    hi = x.astype(BF16)
    r1 = x - hi.astype(F32)
    mid = r1.astype(BF16)
    lo = (r1 - mid.astype(F32)).astype(BF16)
    return hi, mid, lo


def _segsum(x, bd):
    hi = x.astype(BF16)
    lo = (x - hi.astype(F32)).astype(BF16)
    return jnp.dot(hi, bd, preferred_element_type=F32) + jnp.dot(lo, bd, preferred_element_type=F32)


def _sigmoid(x):
    return 1.0 / (1.0 + jnp.exp(-x))


def _ada_kernel(c_ref, w_ref, b_ref, o_ref):
    c = c_ref[...]
    o_ref[...] = _dot(c * _sigmoid(c), w_ref[...]) + b_ref[...]


def _ada_mod(c_all, w_ada, b_ada, *, tn=1024):
    depth, d, n = w_ada.shape
    rows = c_all.shape[0]
    return pl.pallas_call(
        _ada_kernel,
        grid=(depth, n // tn),
        in_specs=[pl.BlockSpec((rows, d), lambda l, j: (0, 0)),
                  pl.BlockSpec((None, d, tn), lambda l, j: (l, 0, j)),
                  pl.BlockSpec((None, 1, tn), lambda l, j: (l, 0, j))],
        out_specs=pl.BlockSpec((None, rows, tn), lambda l, j: (l, 0, j)),
        out_shape=jax.ShapeDtypeStruct((depth, rows, n), F32),
        compiler_params=_cparams(("parallel", "parallel"), 40),
        name="ada_mod",
    )(c_all, w_ada, b_ada.reshape(depth, 1, n))


def _norm_mm_kernel(x_ref, g_ref, sc_ref, sh_ref, w_ref, o_ref, h_ref, *, relu2):
    @pl.when(pl.program_id(1) == 0)
    def _():
        x = x_ref[...]
        y = x * lax.rsqrt(jnp.mean(x * x, axis=-1, keepdims=True) + NORM_EPS) * g_ref[...]
        h_ref[...] = (y * (1.0 + sc_ref[...]) + sh_ref[...]).astype(h_ref.dtype)

    acc = jnp.dot(h_ref[...], w_ref[...], preferred_element_type=F32)
    if relu2:
        acc = jnp.square(jnp.maximum(acc, 0.0))
    o_ref[...] = acc.astype(o_ref.dtype)


def _norm_mm(x, g, sc, sh, w, *, relu2=False, out_dtype=F32, tn):
    rows, d = x.shape
    n = w.shape[1]
    tm = min(rows, 512)
    mod_spec = (pl.BlockSpec((tm, d), lambda i, j: (i, 0)) if sc.shape[0] == rows
                else pl.BlockSpec((1, d), lambda i, j: (0, 0)))
    return pl.pallas_call(
        functools.partial(_norm_mm_kernel, relu2=relu2),
        grid=(rows // tm, n // tn),
        in_specs=[pl.BlockSpec((tm, d), lambda i, j: (i, 0)),
                  pl.BlockSpec((1, d), lambda i, j: (0, 0)),
                  mod_spec, mod_spec,
                  pl.BlockSpec((d, tn), lambda i, j: (0, j))],
        out_specs=pl.BlockSpec((tm, tn), lambda i, j: (i, j)),
        out_shape=jax.ShapeDtypeStruct((rows, n), out_dtype),
        scratch_shapes=[pltpu.VMEM((tm, d), BF16)],
        compiler_params=_cparams(("parallel", "arbitrary"), 48),
        name="norm_mm",
    )(x, g, sc, sh, w)


def _mm_norm_res_kernel(*refs, n_a, nk_each):
    a_refs = refs[:n_a]
    w_ref, g_ref, gate_ref, x_ref, o_ref, acc_ref = refs[n_a:]
    k = pl.program_id(1)

    @pl.when(k == 0)
    def _():
        acc_ref[...] = jnp.zeros_like(acc_ref)

    for idx, a_ref in enumerate(a_refs):
        @pl.when((k >= idx * nk_each) & (k < (idx + 1) * nk_each))
        def _(a_ref=a_ref):
            acc_ref[...] += jnp.dot(a_ref[...], w_ref[...], preferred_element_type=F32)

    @pl.when(k == pl.num_programs(1) - 1)
    def _():
        o = acc_ref[...]
        y = o * lax.rsqrt(jnp.mean(o * o, axis=-1, keepdims=True) + NORM_EPS) * g_ref[...]
        o_ref[...] = x_ref[...] + gate_ref[...] * y


def _mm_norm_res(a_list, w, g, gate, x, *, tk=1024):
    rows, d = x.shape
    tm = min(rows, 512)
    ka = a_list[0].shape[1]
    nk_each = ka // tk
    n_a = len(a_list)
    row_spec = pl.BlockSpec((tm, d), lambda i, k: (i, 0))
    gate_spec = row_spec if gate.shape[0] == rows else pl.BlockSpec((1, d), lambda i, k: (0, 0))

    def a_spec(idx):
        return pl.BlockSpec((tm, tk), lambda i, k: (i, jnp.clip(k - idx * nk_each, 0, nk_each - 1)))

    return pl.pallas_call(
        functools.partial(_mm_norm_res_kernel, n_a=n_a, nk_each=nk_each),
        grid=(rows // tm, n_a * nk_each),
        in_specs=[a_spec(idx) for idx in range(n_a)] + [
            pl.BlockSpec((tk, d), lambda i, k: (k, 0)),
            pl.BlockSpec((1, d), lambda i, k: (0, 0)),
            gate_spec, row_spec],
        out_specs=row_spec,
        out_shape=jax.ShapeDtypeStruct((rows, d), F32),
        scratch_shapes=[pltpu.VMEM((tm, d), F32)],
        compiler_params=_cparams(("parallel", "arbitrary"), 48),
        name="mm_norm_res",
    )(*a_list, w, g, gate, x)


def _pool_kernel(u_ref, prev_ref, hist_ref, w_ref, ps_ref, o_ref, *, tm, gc, pos0):
    i = pl.program_id(1)
    prev = jnp.where(i == 0, hist_ref[...], prev_ref[...])
    full = jnp.concatenate([prev, u_ref[...]], axis=0)
    pos = (pos0 + i * tm + lax.broadcasted_iota(jnp.int32, (tm, gc), 0)).astype(F32)
    for gi, win in enumerate(POOL_WINDOWS):
        f = full[:, gi * gc:(gi + 1) * gc]
        s, d = f, 1
        while d < win:
            s = s + pltpu.roll(s, d, axis=0)
            d *= 2
        cnt = jnp.minimum(F32(win), pos + 1.0)
        dev = s[POOL_PAD:, :] / cnt - f[POOL_PAD:, :]
        o = _dot(dev, w_ref[gi]) * ps_ref[:, gi * gc:(gi + 1) * gc]
        o_ref[:, gi * gc:(gi + 1) * gc] = o.astype(o_ref.dtype)


def _pool_mix(u, hist_pad, w_pool, pool_scale, *, pos0):
    b, t, dp = u.shape
    tm = min(t, 512)
    gc = dp // len(POOL_WINDOWS)
    assert max(POOL_WINDOWS) <= POOL_PAD and tm % POOL_PAD == 0
    blk = tm // POOL_PAD
    return pl.pallas_call(
        functools.partial(_pool_kernel, tm=tm, gc=gc, pos0=pos0),
        grid=(b, t // tm),
        in_specs=[pl.BlockSpec((None, tm, dp), lambda bi, i: (bi, i, 0)),
                  pl.BlockSpec((None, POOL_PAD, dp), lambda bi, i: (bi, jnp.maximum(i * blk - 1, 0), 0)),
                  pl.BlockSpec((None, POOL_PAD, dp), lambda bi, i: (bi, 0, 0)),
                  pl.BlockSpec(w_pool.shape, lambda bi, i: (0, 0, 0)),
                  pl.BlockSpec((1, dp), lambda bi, i: (0, 0))],
        out_specs=pl.BlockSpec((None, tm, dp), lambda bi, i: (bi, i, 0)),
        out_shape=jax.ShapeDtypeStruct((b, t, dp), BF16),
        compiler_params=_cparams(("parallel", "parallel"), 32),
        name="pool_mix",
    )(u, u, hist_pad, w_pool, pool_scale)


def _prep_kernel(z_ref, zprev_ref, zfirst_ref, mu_ref, wl_ref, w0_ref, a0_ref, kk_ref, ka_ref, bd_ref,
                 r_out, lw_out, k_out, v_out, kk_out, b_out, g_out, *, dr):
    i = pl.program_id(1)
    z = z_ref[...]
    prev_row = jnp.where(i == 0, zfirst_ref[7:8, :], zprev_ref[7:8, :])
    rows = lax.broadcasted_iota(jnp.int32, z.shape, 0)
    z_prev = jnp.where(rows == 0, prev_row, pltpu.roll(z, 1, axis=0))
    zs = z + (z_prev - z) * mu_ref[...]
    r, k, v, xl = zs[:, :dr], zs[:, dr:2 * dr], zs[:, 2 * dr:3 * dr], zs[:, 3 * dr:]
    x = w0_ref[...] + _dot(jnp.tanh(xl), wl_ref[0])
    w_log = -(jnp.maximum(-x, 0.0) + jnp.log(1.0 + jnp.exp(-jnp.abs(x)))) - 0.5
    a = _sigmoid(a0_ref[...] + _dot(xl, wl_ref[1]))
    g = _dot(_sigmoid(xl), wl_ref[2])
    kk = k * kk_ref[...]
    kk = kk / jnp.maximum(jnp.sqrt(_segsum(kk * kk, bd_ref[...])), 1e-12)
    r_out[...] = r
    lw_out[...] = -jnp.exp(w_log)
    k_out[...] = k * (1.0 + (a - 1.0) * ka_ref[...])
    v_out[...] = v
    kk_out[...] = kk
    b_out[...] = kk * a
    g_out[...] = g


def _rwkv_prep(z, zfirst, mu, wl, w0, a0, k_k, k_a, bd):
    b, t, ds = z.shape
    dr = w0.shape[1]
    tm = min(t, 256)
    blk = tm // 8
    vec = pl.BlockSpec((1, dr), lambda bi, i: (0, 0))
    tile = pl.BlockSpec((None, tm, dr), lambda bi, i: (bi, i, 0))
    out = jax.ShapeDtypeStruct((b, t, dr), F32)
    return pl.pallas_call(
        functools.partial(_prep_kernel, dr=dr),
        grid=(b, t // tm),
        in_specs=[pl.BlockSpec((None, tm, ds), lambda bi, i: (bi, i, 0)),
                  pl.BlockSpec((None, 8, ds), lambda bi, i: (bi, jnp.maximum(i * blk - 1, 0), 0)),
                  pl.BlockSpec((None, 8, ds), lambda bi, i: (bi, 0, 0)),
                  pl.BlockSpec((1, ds), lambda bi, i: (0, 0)),
                  pl.BlockSpec(wl.shape, lambda bi, i: (0, 0, 0)),
                  vec, vec, vec, vec,
                  pl.BlockSpec(bd.shape, lambda bi, i: (0, 0))],
        out_specs=[tile] * 7,
        out_shape=[out] * 7,
        compiler_params=_cparams(("parallel", "parallel"), 48),
        name="rwkv_prep",
    )(z, z, zfirst, mu, wl, w0, a0, k_k, k_a, bd)


def _wkv_kernel(r_ref, lw_ref, k_ref, v_ref, kk_ref, b_ref, s0_ref, y_ref, sout_ref, s_ref, *, nh, hd):
    c = pl.program_id(1)

    @pl.when(c == 0)
    def _():
        s_ref[...] = s0_ref[...]

    lw = lw_ref[...]
    n_tok = lw.shape[0]
    ti = lax.broadcasted_iota(jnp.int32, (n_tok, n_tok), 0)
    tj = lax.broadcasted_iota(jnp.int32, (n_tok, n_tok), 1)
    tri = (ti >= tj).astype(BF16)
    hi, mid, lo = _split3(lw)
    cum = (jnp.dot(tri, hi, preferred_element_type=F32) + jnp.dot(tri, mid, preferred_element_type=F32)
           + jnp.dot(tri, lo, preferred_element_type=F32))
    tot = cum[n_tok - 1:n_tok, :]
    e_neg = jnp.exp(-cum)
    e_rem = jnp.exp(tot - cum)
    kk, b, k = kk_ref[...], b_ref[...], k_ref[...]

    def heads(x):
        return jnp.stack([x[:, h * hd:(h + 1) * hd] for h in range(nh)])

    aw = heads(-kk * jnp.exp(cum - lw))
    rw = heads(r_ref[...] * jnp.exp(cum))
    bw = heads(b * e_neg)
    kw = heads(k * e_neg)
    bh = heads(b * e_rem)
    kh = heads(k * e_rem)
    v = heads(v_ref[...])
    w_tot = heads(jnp.exp(tot))

    def bmm(spec, x, y):
        return jnp.einsum(spec, x.astype(BF16), y.astype(BF16), preferred_element_type=F32)

    strict = (ti > tj)[None]
    incl = (ti >= tj)[None]
    a_ab = jnp.where(strict, bmm('hlk,hmk->hlm', aw, bw), 0.0)
    a_ak = jnp.where(strict, bmm('hlk,hmk->hlm', aw, kw), 0.0)
    a_rb = jnp.where(incl, bmm('hlk,hmk->hlm', rw, bw), 0.0)
    a_rk = jnp.where(incl, bmm('hlk,hmk->hlm', rw, kw), 0.0)
    x = jnp.concatenate([aw, bmm('hlm,hmv->hlv', a_ak, v)], axis=2)
    npow, span = a_ab, 1
    while True:
        x = x + bmm('hlm,hmv->hlv', npow, x)
        span *= 2
        if span >= n_tok:
            break
        npow = bmm('hlm,hmv->hlv', npow, npow)
    a_bar, u_c = x[:, :, :hd], x[:, :, hd:]

    s = s_ref[...]
    u = bmm('hlk,hvk->hlv', a_bar, s) + u_c
    y = (bmm('hlk,hvk->hlv', rw, s) + bmm('hlm,hmv->hlv', a_rb, u) + bmm('hlm,hmv->hlv', a_rk, v))
    s_new = s * w_tot + bmm('hlv,hlk->hvk', u, bh) + bmm('hlv,hlk->hvk', v, kh)
    s_ref[...] = s_new
    y_ref[...] = jnp.concatenate([y[h] for h in range(nh)], axis=1)

    @pl.when(c == pl.num_programs(1) - 1)
    def _():
        sout_ref[...] = s_new


def _wkv(r, lw, k, v, kk, b, s0):
    bsz, t, dr = r.shape
    nh, hd = s0.shape[1], s0.shape[2]
    n_tok = min(t, CHUNK)
    tile = pl.BlockSpec((None, n_tok, dr), lambda bi, c: (bi, c, 0))
    st = pl.BlockSpec((None, nh, hd, hd), lambda bi, c: (bi, 0, 0, 0))
    return pl.pallas_call(
        functools.partial(_wkv_kernel, nh=nh, hd=hd),
        grid=(bsz, t // n_tok),
        in_specs=[tile] * 6 + [st],
        out_specs=[tile, st],
        out_shape=[jax.ShapeDtypeStruct((bsz, t, dr), F32), jax.ShapeDtypeStruct(s0.shape, F32)],
        scratch_shapes=[pltpu.VMEM((nh, hd, hd), F32)],
        compiler_params=_cparams(("parallel", "arbitrary"), 48),
        name="wkv7",
    )(r, lw, k, v, kk, b, s0)


def _post_kernel(y_ref, r_ref, k_ref, v_ref, g_ref, lw_ref, lb_ref, rk_ref, bd_ref, o_ref, *, hd):
    bd = bd_ref[...]
    y = y_ref[...]
    mu = _segsum(y, bd) * (1.0 / hd)
    yc = y - mu
    var = _segsum(yc * yc, bd) * (1.0 / hd)
    yn = yc * lax.rsqrt(var + LNX_EPS) * lw_ref[...] + lb_ref[...]
    bonus = _segsum(r_ref[...] * k_ref[...] * rk_ref[...], bd) * v_ref[...]
    o_ref[...] = ((yn + bonus) * g_ref[...]).astype(o_ref.dtype)


def _rwkv_post(y, r, k, v, g, lnx_w, lnx_b, r_k, bd):
    rows, dr = y.shape
    tm = min(rows, 512)
    tile = pl.BlockSpec((tm, dr), lambda i: (i, 0))
    vec = pl.BlockSpec((1, dr), lambda i: (0, 0))
    return pl.pallas_call(
        functools.partial(_post_kernel, hd=RWKV_HEAD),
        grid=(rows // tm,),
        in_specs=[tile] * 5 + [vec] * 3 + [pl.BlockSpec(bd.shape, lambda i: (0, 0))],
        out_specs=tile,
        out_shape=jax.ShapeDtypeStruct((rows, dr), BF16),
        compiler_params=_cparams(("parallel",), 48),
        name="rwkv_post",
    )(y, r, k, v, g, lnx_w, lnx_b, r_k, bd)


def _bias_kernel(rb_ref, o_ref, tv_ref, *, tq, nk, width):
    h = pl.program_id(0)
    nrel = rb_ref.shape[1]
    m = lax.broadcasted_iota(jnp.int32, (nrel, width), 1)
    r = lax.broadcasted_iota(jnp.int32, (nrel, width), 0)
    m = jnp.where(m >= nk, m - width, m)
    sel = (jnp.clip(PREV_ROWS - m, -REL_CLIP, REL_CLIP) + REL_CLIP == r).astype(F32)
    tv_ref[...] = jnp.dot(rb_ref[...], sel, precision=lax.Precision.HIGHEST, preferred_element_type=F32)
    row = jnp.broadcast_to(tv_ref[pl.ds(h, 1), :], (tq, width))
    t = pltpu.roll(row, 0, axis=1, stride=1, stride_axis=0)[:, :nk]
    shift = CHUNK.bit_length() - 1
    ci = lax.shift_right_logical(lax.broadcasted_iota(jnp.int32, (tq, nk), 0), shift)
    cj = lax.shift_right_logical(lax.broadcasted_iota(jnp.int32, (tq, nk), 1), shift)
    o_ref[...] = jnp.where((cj >= ci) & (cj <= ci + N_PREV_CHUNKS), t, NEG_INF)


def _bias_tile(rel_bias, *, tq):
    nh, nrel = rel_bias.shape
    nk = PREV_ROWS + tq
    width = -(-(nk + tq) // LANES) * LANES
    nrel_pad = -(-nrel // LANES) * LANES
    rb = jnp.pad(rel_bias, ((0, 0), (0, nrel_pad - nrel)))
    return pl.pallas_call(
        functools.partial(_bias_kernel, tq=tq, nk=nk, width=width),
        grid=(nh,),
        in_specs=[pl.BlockSpec((nh, nrel_pad), lambda h: (0, 0))],
        out_specs=pl.BlockSpec((None, tq, nk), lambda h: (h, 0, 0)),
        out_shape=jax.ShapeDtypeStruct((nh, tq, nk), F32),
        scratch_shapes=[pltpu.VMEM((nh, width), F32)],
        compiler_params=_cparams(("arbitrary",), 32),
        name="band_bias",
    )(rb)


def _attn_kernel(*refs, nkb, scale, mask_axis):
    q_ref = refs[0]
    k_refs, v_refs = refs[1:1 + nkb], refs[1 + nkb:1 + 2 * nkb]
    bias_ref, o_ref = refs[1 + 2 * nkb:]
    k = jnp.concatenate([kr[...].astype(BF16) for kr in k_refs], axis=0)
    v = jnp.concatenate([vr[...].astype(BF16) for vr in v_refs], axis=0)
    s = _dot_nt(q_ref[...], k) * scale + bias_ref[...]
    if mask_axis is not None:
        tq = q_ref.shape[0]
        first = pl.program_id(mask_axis) * tq - PREV_ROWS
        col = lax.broadcasted_iota(jnp.int32, s.shape, 1)
        s = jnp.where(col + first >= 0, s, NEG_INF)
    p = jnp.exp(s - jnp.max(s, axis=-1, keepdims=True))
    l = jnp.sum(p, axis=-1, keepdims=True)
    o_ref[...] = (jnp.dot(p.astype(BF16), v, preferred_element_type=F32) / l).astype(o_ref.dtype)


def _attn_prompt(qkv, bias, *, nh, dh, tq=256):
    t = qkv.shape[0]
    assert tq % CHUNK == 0 and PREV_ROWS % tq == 0
    nkb = PREV_ROWS // tq + 1

    def kv_spec(which, back):
        return pl.BlockSpec((tq, dh), lambda h, qb: (jnp.maximum(qb - back, 0), which * nh + h))

    return pl.pallas_call(
        functools.partial(_attn_kernel, nkb=nkb, scale=dh ** -0.5, mask_axis=1),
        grid=(nh, t // tq),
        in_specs=[pl.BlockSpec((tq, dh), lambda h, qb: (qb, h))]
                 + [kv_spec(1, nkb - 1 - j) for j in range(nkb)]
                 + [kv_spec(2, nkb - 1 - j) for j in range(nkb)]
                 + [pl.BlockSpec((None, tq, PREV_ROWS + tq), lambda h, qb: (h, 0, 0))],
        out_specs=pl.BlockSpec((tq, dh), lambda h, qb: (qb, h)),
        out_shape=jax.ShapeDtypeStruct((t, nh * dh), BF16),
        compiler_params=_cparams(("parallel", "parallel"), 32),
        name="band_attn_prompt",
    )(qkv, *([qkv] * (2 * nkb)), bias)


def _attn_sample(q, kcat, vcat, bias, *, nh, dh):
    b, t, _ = q.shape
    nk = kcat.shape[1]
    kv = pl.BlockSpec((None, nk, dh), lambda h, bi: (bi, 0, h))
    return pl.pallas_call(
        functools.partial(_attn_kernel, nkb=1, scale=dh ** -0.5, mask_axis=None),
        grid=(nh, b),
        in_specs=[pl.BlockSpec((None, t, dh), lambda h, bi: (bi, 0, h)), kv, kv,
                  pl.BlockSpec((None, t, nk), lambda h, bi: (h, 0, 0))],
        out_specs=pl.BlockSpec((None, t, dh), lambda h, bi: (bi, 0, h)),
        out_shape=jax.ShapeDtypeStruct((b, t, nh * dh), BF16),
        compiler_params=_cparams(("parallel", "parallel"), 32),
        name="band_attn_sample",
    )(q, kcat, vcat, bias)


def _run_group(x3, mods, pool_hist, shift_prev, wkv0, cache, pos0, p):
    bsz, t, d = x3.shape
    rows = bsz * t
    dp = pool_hist.shape[-1]
    dr = d - dp
    nh_att = p['rel_bias'].shape[0]
    dh = d // nh_att
    x = x3.reshape(rows, d)

    sh1, sc1, gt1, sh2, sc2, gt2 = mods[0]
    gn = p['g_norm'][0]
    u = _norm_mm(x, gn[0:1], sc1, sh1, p['w_in_u'], tn=512).reshape(bsz, t, dp)
    z = _norm_mm(x, gn[0:1], sc1, sh1, p['w_in_z'], tn=256).reshape(bsz, t, -1)
    assert t >= POOL_PAD - 1
    hist_pad = jnp.pad(pool_hist, ((0, 0), (POOL_PAD - pool_hist.shape[1], 0), (0, 0)))
    pool_out = _pool_mix(u, hist_pad, p['w_pool'], p['pool_scale'], pos0=pos0)
    zfirst = jnp.pad(shift_prev[:, None, :], ((0, 0), (7, 0), (0, 0)))
    r, lw, kmod, v, kk, b, g = _rwkv_prep(z, zfirst, p['mu_shift'], p['w_lora'], p['w0_decay'], p['a0_iclr'],
                                          p['k_k'], p['k_a'], p['bd'])
    y, wkv_new = _wkv(r, lw, kmod, v, kk, b, wkv0)
    flat = lambda a: a.reshape(rows, -1)
    yg = _rwkv_post(flat(y), flat(r), flat(kmod), flat(v), flat(g), p['lnx_w'], p['lnx_b'], p['r_k'], p['bd'])
    x = _mm_norm_res([flat(pool_out), yg], p['w_out0'], gn[1:2], gt1, x)
    a = _norm_mm(x, gn[2:3], sc2, sh2, p['w_ff1'][0], relu2=True, out_dtype=BF16, tn=512)
    x = _mm_norm_res([a], p['w_ff2'][0], gn[3:4], gt2, x)
    pool_new = u[:, t - (POOL_PAD - 1):]
    shift_new = z[:, t - 1]

    sh1, sc1, gt1, sh2, sc2, gt2 = mods[1]
    gn = p['g_norm'][1]
    qkv = _norm_mm(x, gn[0:1], sc1, sh1, p['w_qkv1'], tn=512)
    if cache is None:
        assert bsz == 1 and pos0 == 0
        o = _attn_prompt(qkv, p['bias_tile'], nh=nh_att, dh=dh)
        keep = min(PREV_ROWS, t)
        k_new = qkv[t - keep:, d:2 * d].reshape(1, keep, nh_att, dh)
        v_new = qkv[t - keep:, 2 * d:].reshape(1, keep, nh_att, dh)
    else:
        cache_k, cache_v = cache
        nr = cache_k.shape[1]
        assert nr == PREV_ROWS and pos0 % CHUNK == 0 and pos0 >= nr and t <= CHUNK
        qkv3 = qkv.reshape(bsz, t, 3 * d)
        k_new, v_new = qkv3[:, :, d:2 * d], qkv3[:, :, 2 * d:]
        kcat = jnp.concatenate([cache_k.reshape(bsz, nr, d), k_new], axis=1)
        vcat = jnp.concatenate([cache_v.reshape(bsz, nr, d), v_new], axis=1)
        o = _attn_sample(qkv3[:, :, :d], kcat, vcat, p['bias_tile'][:, :t, :nr + t], nh=nh_att, dh=dh)
        o = o.reshape(rows, d)
        k_new = k_new.reshape(bsz, t, nh_att, dh)
        v_new = v_new.reshape(bsz, t, nh_att, dh)
    x = _mm_norm_res([o], p['w_out1'], gn[1:2], gt1, x)
    a = _norm_mm(x, gn[2:3], sc2, sh2, p['w_ff1'][1], relu2=True, out_dtype=BF16, tn=512)
    x = _mm_norm_res([a], p['w_ff2'][1], gn[3:4], gt2, x)
    return x.reshape(bsz, t, d), pool_new, shift_new, wkv_new, k_new, v_new


def kernel(x_prompt, x_sample, c_prompt, c_sample, state_l0_pool, state_l0_shift, state_l0_wkv, cache_l1_k, cache_l1_v, w_ada, b_ada, g_norm, w_in0, w_pool, pool_scale, mu_shift, w0_decay, w2_decay, a0_iclr, a2_iclr, g2_gate, k_k, k_a, r_k, lnx_w, lnx_b, w_out0, w_qkv1, rel_bias, w_out1, w_ff1, w_ff2):
    bp, tp, d = x_prompt.shape
    bs, ts, _ = x_sample.shape
    depth = w_ada.shape[0]
    dp = state_l0_pool.shape[-1]
    dr = d - dp
    nh = dr // RWKV_HEAD
    n_dec, n_iclr, n_gate = w2_decay.shape[0], a2_iclr.shape[0], g2_gate.shape[0]
    row = lambda a: a.reshape(1, -1)

    n_c = bp + bs
    c_all = jnp.pad(jnp.concatenate([c_prompt, c_sample], axis=0), ((0, -n_c % 8), (0, 0)))
    mod = _ada_mod(c_all, w_ada, b_ada)
    mods_p = [[mod[l, 0:bp, i * d:(i + 1) * d] for i in range(6)] for l in range(depth)]
    mods_s = [[jnp.repeat(mod[l, bp:n_c, i * d:(i + 1) * d], ts, axis=0) for i in range(6)] for l in range(depth)]

    hid = lax.broadcasted_iota(jnp.int32, (dr, dr), 0) // RWKV_HEAD
    hjd = lax.broadcasted_iota(jnp.int32, (dr, dr), 1) // RWKV_HEAD
    w_lora = jnp.zeros((3, n_dec + n_iclr + n_gate, dr), F32)
    w_lora = w_lora.at[0, :n_dec].set(w2_decay).at[1, n_dec:n_dec + n_iclr].set(a2_iclr)
    w_lora = w_lora.at[2, n_dec + n_iclr:].set(g2_gate)
    p = {
        'g_norm': g_norm,
        'w_in_u': w_in0[:, :dp].astype(BF16), 'w_in_z': w_in0[:, dp:].astype(BF16),
        'w_pool': w_pool.astype(BF16), 'pool_scale': row(pool_scale), 'mu_shift': row(mu_shift),
        'w_lora': w_lora.astype(BF16), 'w0_decay': row(w0_decay), 'a0_iclr': row(a0_iclr),
        'k_k': row(k_k), 'k_a': row(k_a), 'r_k': row(r_k), 'lnx_w': row(lnx_w), 'lnx_b': row(lnx_b),
        'bd': (hid == hjd).astype(BF16),
        'w_out0': w_out0.astype(BF16), 'w_qkv1': w_qkv1.astype(BF16), 'w_out1': w_out1.astype(BF16),
        'w_ff1': w_ff1.astype(BF16), 'w_ff2': w_ff2.astype(BF16),
        'rel_bias': rel_bias, 'bias_tile': _bias_tile(rel_bias, tq=256),
    }

    y_p, pool_p, shift_p, wkv_p, k_p, v_p = _run_group(
        x_prompt, mods_p, jnp.zeros((bp, POOL_PAD - 1, dp), F32), jnp.zeros((bp, w_in0.shape[1] - dp), F32),
        jnp.zeros((bp, nh, RWKV_HEAD, RWKV_HEAD), F32), None, 0, p)
    y_s, pool_s, shift_s, wkv_s, k_s, v_s = _run_group(
        x_sample, mods_s, state_l0_pool, state_l0_shift, state_l0_wkv, (cache_l1_k, cache_l1_v), PAST_LEN, p)
    return (y_p, y_s, pool_p, pool_s, shift_p, shift_s, wkv_p, wkv_s, k_p, v_p, k_s, v_s)
```

```python
import functools

import jax
import jax.numpy as jnp
from jax import lax
from jax.experimental import pallas as pl
from jax.experimental.pallas import tpu as pltpu

F32 = jnp.float32
BF16 = jnp.bfloat16

NORM_EPS = 1e-6
LNX_EPS = 64e-5
CHUNK = 64
POOL_WINDOWS = (2, 4, 8, 16)
POOL_PAD = 16
RWKV_HEAD = 64
N_PREV_CHUNKS = 8
PREV_ROWS = N_PREV_CHUNKS * CHUNK
REL_CLIP = 2 * CHUNK
PAST_LEN = 2048
NEG_INF = -1e30
V7X_VMEM_BYTES = 64 * 1024 * 1024
LANES = 128
SUBLANES = 8
ATT_TQ = 256
ATT_HEADS_PER_STEP = 2


def _cparams(semantics, vmem_mb):
    assert vmem_mb * 1024 * 1024 <= V7X_VMEM_BYTES
    return pltpu.CompilerParams(dimension_semantics=semantics, vmem_limit_bytes=vmem_mb * 1024 * 1024)


def _col_tile(n, cap):
    best = None
    for t in range(LANES, min(n, cap) + 1, LANES):
        if n % t == 0:
            best = t
    assert best is not None
    return best


def _dot(a, b):
    return jnp.dot(a.astype(BF16), b.astype(BF16), preferred_element_type=F32)


def _dot_nt(a, b):
    return lax.dot_general(a.astype(BF16), b.astype(BF16), (((1,), (1,)), ((), ())), preferred_element_type=F32)


def _split2(x):
    hi = x.astype(BF16)
    return hi, (x - hi.astype(F32)).astype(BF16)


def _split3(x):
    hi = x.astype(BF16)
    r1 = x - hi.astype(F32)
    mid = r1.astype(BF16)
    lo = (r1 - mid.astype(F32)).astype(BF16)
    return hi, mid, lo


def _segsum(x, red, exp):
    hi, lo = _split2(jnp.dot(x.astype(BF16), red, preferred_element_type=F32))
    return jnp.dot(hi, exp, preferred_element_type=F32) + jnp.dot(lo, exp, preferred_element_type=F32)


def _sigmoid(x):
    return 1.0 / (1.0 + jnp.exp(-x))


def _ada_kernel(c_ref, w_ref, b_ref, o_ref):
    c = c_ref[...]
    o_ref[...] = _dot(c * _sigmoid(c), w_ref[...]) + b_ref[...]


def _ada_mod(c_all, w_ada, b_ada, *, tn=1024):
    depth, d, n = w_ada.shape
    rows = c_all.shape[0]
    return pl.pallas_call(
        _ada_kernel,
        grid=(depth, n // tn),
        in_specs=[pl.BlockSpec((rows, d), lambda l, j: (0, 0)),
                  pl.BlockSpec((None, d, tn), lambda l, j: (l, 0, j)),
                  pl.BlockSpec((None, 1, tn), lambda l, j: (l, 0, j))],
        out_specs=pl.BlockSpec((None, rows, tn), lambda l, j: (l, 0, j)),
        out_shape=jax.ShapeDtypeStruct((depth, rows, n), F32),
        compiler_params=_cparams(("parallel", "parallel"), 40),
        name="ada_mod",
    )(c_all, w_ada, b_ada.reshape(depth, 1, n))


def _norm_mm_kernel(x_ref, g_ref, sc_ref, sh_ref, w_ref, o_ref, h_ref, *, relu2):
    @pl.when(pl.program_id(1) == 0)
    def _():
        x = x_ref[...]
        y = x * lax.rsqrt(jnp.mean(x * x, axis=-1, keepdims=True) + NORM_EPS) * g_ref[...]
        h_ref[...] = (y * (1.0 + sc_ref[...]) + sh_ref[...]).astype(h_ref.dtype)

    acc = jnp.dot(h_ref[...], w_ref[...], preferred_element_type=F32)
    if relu2:
        acc = jnp.square(jnp.maximum(acc, 0.0))
    o_ref[...] = acc.astype(o_ref.dtype)


def _norm_mm(x, g, sc, sh, w, *, relu2=False, out_dtype=F32, tm=1024, tn_cap=1024):
    rows, d = x.shape
    n = w.shape[1]
    tm = min(rows, tm)
    tn = _col_tile(n, tn_cap if rows > tm else 2 * tn_cap)
    mod_spec = (pl.BlockSpec((tm, d), lambda i, j: (i, 0)) if sc.shape[0] == rows
                else pl.BlockSpec((1, d), lambda i, j: (0, 0)))
    return pl.pallas_call(
        functools.partial(_norm_mm_kernel, relu2=relu2),
        grid=(rows // tm, n // tn),
        in_specs=[pl.BlockSpec((tm, d), lambda i, j: (i, 0)),
                  pl.BlockSpec((1, d), lambda i, j: (0, 0)),
                  mod_spec, mod_spec,
                  pl.BlockSpec((d, tn), lambda i, j: (0, j))],
        out_specs=pl.BlockSpec((tm, tn), lambda i, j: (i, j)),
        out_shape=jax.ShapeDtypeStruct((rows, n), out_dtype),
        scratch_shapes=[pltpu.VMEM((tm, d), BF16)],
        compiler_params=_cparams(("parallel", "arbitrary"), 56),
        name="norm_mm",
    )(x, g, sc, sh, w)


def _mm_norm_res_kernel(*refs, n_a, nk_each):
    a_refs = refs[:n_a]
    w_ref, g_ref, gate_ref, x_ref, o_ref, acc_ref = refs[n_a:]
    k = pl.program_id(1)

    @pl.when(k == 0)
    def _():
        acc_ref[...] = jnp.zeros_like(acc_ref)

    for idx, a_ref in enumerate(a_refs):
        @pl.when((k >= idx * nk_each) & (k < (idx + 1) * nk_each))
        def _(a_ref=a_ref):
            acc_ref[...] += jnp.dot(a_ref[...], w_ref[...], preferred_element_type=F32)

    @pl.when(k == pl.num_programs(1) - 1)
    def _():
        o = acc_ref[...]
        y = o * lax.rsqrt(jnp.mean(o * o, axis=-1, keepdims=True) + NORM_EPS) * g_ref[...]
        o_ref[...] = x_ref[...] + gate_ref[...] * y


def _mm_norm_res(a_list, w, g, gate, x, *, tk_cap=2048):
    rows, d = x.shape
    tm = min(rows, 512)
    ka = a_list[0].shape[1]
    tk = min(ka, tk_cap)
    nk_each = ka // tk
    n_a = len(a_list)
    row_spec = pl.BlockSpec((tm, d), lambda i, k: (i, 0))
    gate_spec = row_spec if gate.shape[0] == rows else pl.BlockSpec((1, d), lambda i, k: (0, 0))

    def a_spec(idx):
        return pl.BlockSpec((tm, tk), lambda i, k: (i, jnp.clip(k - idx * nk_each, 0, nk_each - 1)))

    return pl.pallas_call(
        functools.partial(_mm_norm_res_kernel, n_a=n_a, nk_each=nk_each),
        grid=(rows // tm, n_a * nk_each),
        in_specs=[a_spec(idx) for idx in range(n_a)] + [
            pl.BlockSpec((tk, d), lambda i, k: (k, 0)),
            pl.BlockSpec((1, d), lambda i, k: (0, 0)),
            gate_spec, row_spec],
        out_specs=row_spec,
        out_shape=jax.ShapeDtypeStruct((rows, d), F32),
        scratch_shapes=[pltpu.VMEM((tm, d), F32)],
        compiler_params=_cparams(("parallel", "arbitrary"), 56),
        name="mm_norm_res",
    )(*a_list, w, g, gate, x)


def _pool_kernel(*refs, tm, pos0):
    ng = len(POOL_WINDOWS)
    u_refs, prev_refs = refs[:ng], refs[ng:2 * ng]
    hist_ref, w_ref, ps_ref, o_ref = refs[2 * ng:]
    i = pl.program_id(1)
    gc = u_refs[0].shape[1]
    pos = (pos0 + i * tm + lax.broadcasted_iota(jnp.int32, (tm, gc), 0)).astype(F32)
    for gi, win in enumerate(POOL_WINDOWS):
        prev = jnp.where(i == 0, hist_ref[:, gi * gc:(gi + 1) * gc], prev_refs[gi][...])
        u = u_refs[gi][...]
        s, d = jnp.concatenate([prev, u], axis=0), 1
        while d < win:
            s = s + pltpu.roll(s, d, axis=0)
            d *= 2
        cnt = jnp.minimum(F32(win), pos + 1.0)
        dev = s[POOL_PAD:, :] / cnt - u
        o = _dot(dev, w_ref[gi]) * ps_ref[:, gi * gc:(gi + 1) * gc]
        o_ref[:, gi * gc:(gi + 1) * gc] = o.astype(o_ref.dtype)


def _pool_mix(p_all, col0, hist_pad, w_pool, pool_scale, *, bsz, t, pos0):
    dp = hist_pad.shape[-1]
    ng = len(POOL_WINDOWS)
    gc = dp // ng
    tm = min(t, 512)
    assert max(POOL_WINDOWS) <= POOL_PAD and tm % POOL_PAD == 0 and col0 % gc == 0
    nt, blk, cb = t // tm, tm // POOL_PAD, col0 // gc
    cur = [pl.BlockSpec((tm, gc), lambda b, i, gi=gi: (b * nt + i, cb + gi)) for gi in range(ng)]
    prev = [pl.BlockSpec((POOL_PAD, gc), lambda b, i, gi=gi: (b * nt * blk + jnp.maximum(i * blk - 1, 0), cb + gi))
            for gi in range(ng)]
    return pl.pallas_call(
        functools.partial(_pool_kernel, tm=tm, pos0=pos0),
        grid=(bsz, nt),
        in_specs=cur + prev + [pl.BlockSpec((None, POOL_PAD, dp), lambda b, i: (b, 0, 0)),
                               pl.BlockSpec(w_pool.shape, lambda b, i: (0, 0, 0)),
                               pl.BlockSpec((1, dp), lambda b, i: (0, 0))],
        out_specs=pl.BlockSpec((tm, dp), lambda b, i: (b * nt + i, 0)),
        out_shape=jax.ShapeDtypeStruct((bsz * t, dp), BF16),
        compiler_params=_cparams(("parallel", "parallel"), 32),
        name="pool_mix",
    )(*([p_all] * (2 * ng)), hist_pad, w_pool, pool_scale)


def _prep_kernel(z_ref, zprev_ref, zfirst_ref, mu_ref, wl_ref, w0_ref, a0_ref, kk_ref, ka_ref, red_ref, exp_ref,
                 r_out, lw_out, k_out, v_out, kk_out, b_out, g_out, *, dr):
    i = pl.program_id(1)
    z = z_ref[...]
    last = SUBLANES - 1
    prev_row = jnp.where(i == 0, zfirst_ref[last:, :], zprev_ref[last:, :])
    rows = lax.broadcasted_iota(jnp.int32, z.shape, 0)
    z_prev = jnp.where(rows == 0, prev_row, pltpu.roll(z, 1, axis=0))
    zs = z + (z_prev - z) * mu_ref[...]
    r, k, v, xl = zs[:, :dr], zs[:, dr:2 * dr], zs[:, 2 * dr:3 * dr], zs[:, 3 * dr:]
    x = w0_ref[...] + _dot(jnp.tanh(xl), wl_ref[0])
    w_log = -(jnp.maximum(-x, 0.0) + jnp.log(1.0 + jnp.exp(-jnp.abs(x)))) - 0.5
    a = _sigmoid(a0_ref[...] + _dot(xl, wl_ref[1]))
    g = _dot(_sigmoid(xl), wl_ref[2])
    kk = k * kk_ref[...]
    kk = kk / jnp.maximum(jnp.sqrt(_segsum(kk * kk, red_ref[...], exp_ref[...])), 1e-12)
    r_out[...] = r
    lw_out[...] = -jnp.exp(w_log)
    k_out[...] = k * (1.0 + (a - 1.0) * ka_ref[...])
    v_out[...] = v
    kk_out[...] = kk
    b_out[...] = kk * a
    g_out[...] = g


def _rwkv_prep(p_all, zfirst, mu, wl, w0, a0, k_k, k_a, red, exp, *, bsz, t):
    ds = zfirst.shape[-1]
    dr = w0.shape[1]
    tm = min(t, 256)
    nt, blk = t // tm, tm // SUBLANES
    vec = pl.BlockSpec((1, dr), lambda b, i: (0, 0))
    tile = pl.BlockSpec((tm, dr), lambda b, i: (b * nt + i, 0))
    out = jax.ShapeDtypeStruct((bsz * t, dr), F32)
    return pl.pallas_call(
        functools.partial(_prep_kernel, dr=dr),
        grid=(bsz, nt),
        in_specs=[pl.BlockSpec((tm, ds), lambda b, i: (b * nt + i, 0)),
                  pl.BlockSpec((SUBLANES, ds), lambda b, i: (b * nt * blk + jnp.maximum(i * blk - 1, 0), 0)),
                  pl.BlockSpec((None, SUBLANES, ds), lambda b, i: (b, 0, 0)),
                  pl.BlockSpec((1, ds), lambda b, i: (0, 0)),
                  pl.BlockSpec(wl.shape, lambda b, i: (0, 0, 0)),
                  vec, vec, vec, vec,
                  pl.BlockSpec(red.shape, lambda b, i: (0, 0)),
                  pl.BlockSpec(exp.shape, lambda b, i: (0, 0))],
        out_specs=[tile] * 7,
        out_shape=[out] * 7,
        compiler_params=_cparams(("parallel", "parallel"), 48),
        name="rwkv_prep",
    )(p_all, p_all, zfirst, mu, wl, w0, a0, k_k, k_a, red, exp)


def _wkv_kernel(r_ref, lw_ref, k_ref, v_ref, kk_ref, b_ref, s0_ref, y_ref, sout_ref, s_ref, *, nh, hd):
    c = pl.program_id(1)

    @pl.when(c == 0)
    def _():
        s_ref[...] = s0_ref[...]

    lw = lw_ref[...]
    n_tok = lw.shape[0]
    ti = lax.broadcasted_iota(jnp.int32, (n_tok, n_tok), 0)
    tj = lax.broadcasted_iota(jnp.int32, (n_tok, n_tok), 1)
    tri = (ti >= tj).astype(BF16)
    hi, mid, lo = _split3(lw)
    cum = (jnp.dot(tri, hi, preferred_element_type=F32) + jnp.dot(tri, mid, preferred_element_type=F32)
           + jnp.dot(tri, lo, preferred_element_type=F32))
    tot = cum[n_tok - 1:n_tok, :]
    e_neg = jnp.exp(-cum)
    e_rem = jnp.exp(tot - cum)
    kk, b, k = kk_ref[...], b_ref[...], k_ref[...]

    def heads(x):
        return jnp.stack([x[:, h * hd:(h + 1) * hd] for h in range(nh)])

    aw = heads(-kk * jnp.exp(cum - lw))
    rw = heads(r_ref[...] * jnp.exp(cum))
    bw = heads(b * e_neg)
    kw = heads(k * e_neg)
    bh = heads(b * e_rem)
    kh = heads(k * e_rem)
    v = heads(v_ref[...])
    w_tot = heads(jnp.exp(tot))

    def bmm(spec, x, y):
        return jnp.einsum(spec, x.astype(BF16), y.astype(BF16), preferred_element_type=F32)

    strict = (ti > tj)[None]
    incl = (ti >= tj)[None]
    a_ab = jnp.where(strict, bmm('hlk,hmk->hlm', aw, bw), 0.0)
    a_ak = jnp.where(strict, bmm('hlk,hmk->hlm', aw, kw), 0.0)
    a_rb = jnp.where(incl, bmm('hlk,hmk->hlm', rw, bw), 0.0)
    a_rk = jnp.where(incl, bmm('hlk,hmk->hlm', rw, kw), 0.0)
    x = jnp.concatenate([aw, bmm('hlm,hmv->hlv', a_ak, v)], axis=2)
    npow, span = a_ab, 1
    while True:
        x = x + bmm('hlm,hmv->hlv', npow, x)
        span *= 2
        if span >= n_tok:
            break
        npow = bmm('hlm,hmv->hlv', npow, npow)
    a_bar, u_c = x[:, :, :hd], x[:, :, hd:]

    s = s_ref[...]
    u = bmm('hlk,hvk->hlv', a_bar, s) + u_c
    y = (bmm('hlk,hvk->hlv', rw, s) + bmm('hlm,hmv->hlv', a_rb, u) + bmm('hlm,hmv->hlv', a_rk, v))
    s_new = s * w_tot + bmm('hlv,hlk->hvk', u, bh) + bmm('hlv,hlk->hvk', v, kh)
    s_ref[...] = s_new
    y_ref[...] = jnp.concatenate([y[h] for h in range(nh)], axis=1)

    @pl.when(c == pl.num_programs(1) - 1)
    def _():
        sout_ref[...] = s_new


def _wkv(r, lw, k, v, kk, b, s0, *, t):
    dr = r.shape[1]
    bsz, nh, hd = s0.shape[0], s0.shape[1], s0.shape[2]
    n_tok = min(t, CHUNK)
    nc = t // n_tok
    tile = pl.BlockSpec((n_tok, dr), lambda bi, c: (bi * nc + c, 0))
    st = pl.BlockSpec((None, nh, hd, hd), lambda bi, c: (bi, 0, 0, 0))
    return pl.pallas_call(
        functools.partial(_wkv_kernel, nh=nh, hd=hd),
        grid=(bsz, nc),
        in_specs=[tile] * 6 + [st],
        out_specs=[tile, st],
        out_shape=[jax.ShapeDtypeStruct(r.shape, F32), jax.ShapeDtypeStruct(s0.shape, F32)],
        scratch_shapes=[pltpu.VMEM((nh, hd, hd), F32)],
        compiler_params=_cparams(("parallel", "arbitrary"), 48),
        name="wkv7",
    )(r, lw, k, v, kk, b, s0)


def _post_kernel(y_ref, r_ref, k_ref, v_ref, g_ref, lw_ref, lb_ref, rk_ref, red_ref, exp_ref, o_ref, *, hd):
    red, exp = red_ref[...], exp_ref[...]
    y = y_ref[...]
    mu = _segsum(y, red, exp) * (1.0 / hd)
    yc = y - mu
    var = _segsum(yc * yc, red, exp) * (1.0 / hd)
    yn = yc * lax.rsqrt(var + LNX_EPS) * lw_ref[...] + lb_ref[...]
    bonus = _segsum(r_ref[...] * k_ref[...] * rk_ref[...], red, exp) * v_ref[...]
    o_ref[...] = ((yn + bonus) * g_ref[...]).astype(o_ref.dtype)


def _rwkv_post(y, r, k, v, g, lnx_w, lnx_b, r_k, red, exp):
    rows, dr = y.shape
    tm = min(rows, 512)
    tile = pl.BlockSpec((tm, dr), lambda i: (i, 0))
    vec = pl.BlockSpec((1, dr), lambda i: (0, 0))
    return pl.pallas_call(
        functools.partial(_post_kernel, hd=RWKV_HEAD),
        grid=(rows // tm,),
        in_specs=[tile] * 5 + [vec] * 3 + [pl.BlockSpec(red.shape, lambda i: (0, 0)),
                                           pl.BlockSpec(exp.shape, lambda i: (0, 0))],
        out_specs=tile,
        out_shape=jax.ShapeDtypeStruct((rows, dr), BF16),
        compiler_params=_cparams(("parallel",), 48),
        name="rwkv_post",
    )(y, r, k, v, g, lnx_w, lnx_b, r_k, red, exp)


def _bias_kernel(rb_ref, o_ref, tv_ref, *, tq, nk, width):
    h, var = pl.program_id(0), pl.program_id(1)
    nrel = rb_ref.shape[1]
    m = lax.broadcasted_iota(jnp.int32, (nrel, width), 1)
    r = lax.broadcasted_iota(jnp.int32, (nrel, width), 0)
    m = jnp.where(m >= nk, m - width, m)
    sel = (jnp.clip(PREV_ROWS - m, -REL_CLIP, REL_CLIP) + REL_CLIP == r).astype(F32)
    tv_ref[...] = jnp.dot(rb_ref[...], sel, precision=lax.Precision.HIGHEST, preferred_element_type=F32)
    row = jnp.broadcast_to(tv_ref[pl.ds(h, 1), :], (tq, width))
    t = pltpu.roll(row, 0, axis=1, stride=1, stride_axis=0)[:, :nk]
    shift = CHUNK.bit_length() - 1
    col = lax.broadcasted_iota(jnp.int32, (tq, nk), 1)
    ci = lax.shift_right_logical(lax.broadcasted_iota(jnp.int32, (tq, nk), 0), shift)
    cj = lax.shift_right_logical(col, shift)
    ok = (cj >= ci) & (cj <= ci + N_PREV_CHUNKS) & (col + var * tq - PREV_ROWS >= 0)
    o_ref[...] = jnp.where(ok, t, NEG_INF)


def _bias_tile(rel_bias, *, tq):
    nh, nrel = rel_bias.shape
    nk = PREV_ROWS + tq
    n_var = PREV_ROWS // tq + 1
    width = -(-(nk + tq) // LANES) * LANES
    nrel_pad = -(-nrel // LANES) * LANES
    rb = jnp.pad(rel_bias, ((0, 0), (0, nrel_pad - nrel)))
    return pl.pallas_call(
        functools.partial(_bias_kernel, tq=tq, nk=nk, width=width),
        grid=(nh, n_var),
        in_specs=[pl.BlockSpec((nh, nrel_pad), lambda h, v: (0, 0))],
        out_specs=pl.BlockSpec((None, None, tq, nk), lambda h, v: (h, v, 0, 0)),
        out_shape=jax.ShapeDtypeStruct((nh, n_var, tq, nk), F32),
        scratch_shapes=[pltpu.VMEM((nh, width), F32)],
        compiler_params=_cparams(("arbitrary", "arbitrary"), 32),
        name="band_bias",
    )(rb)


def _softmax_parts(s):
    p = jnp.exp(s - jnp.max(s, axis=-1, keepdims=True))
    return p.astype(BF16), jnp.sum(p, axis=-1, keepdims=True)


def _attn_prompt_kernel(*refs, nkb, dh):
    q_ref = refs[0]
    k_refs, v_refs = refs[1:1 + nkb], refs[1 + nkb:1 + 2 * nkb]
    bias_ref, o_ref = refs[1 + 2 * nkb:]
    for hh in range(ATT_HEADS_PER_STEP):
        cols = slice(hh * dh, (hh + 1) * dh)
        k = jnp.concatenate([kr[:, cols] for kr in k_refs], axis=0)
        v = jnp.concatenate([vr[:, cols] for vr in v_refs], axis=0)
        p, l = _softmax_parts(_dot_nt(q_ref[:, cols], k) + bias_ref[hh])
        o_ref[:, cols] = (jnp.dot(p, v, preferred_element_type=F32) / l).astype(o_ref.dtype)


def _attn_prompt(qkv, bias, *, nh, dh):
    t = qkv.shape[0]
    tq, hs = ATT_TQ, ATT_HEADS_PER_STEP
    assert tq % CHUNK == 0 and PREV_ROWS % tq == 0 and nh % hs == 0
    nkb = PREV_ROWS // tq + 1
    nhp = nh // hs

    def kv_spec(which, back):
        return pl.BlockSpec((tq, hs * dh), lambda h, qb: (jnp.maximum(qb - back, 0), which * nhp + h))

    return pl.pallas_call(
        functools.partial(_attn_prompt_kernel, nkb=nkb, dh=dh),
        grid=(nhp, t // tq),
        in_specs=[pl.BlockSpec((tq, hs * dh), lambda h, qb: (qb, h))]
                 + [kv_spec(1, nkb - 1 - j) for j in range(nkb)]
                 + [kv_spec(2, nkb - 1 - j) for j in range(nkb)]
                 + [pl.BlockSpec((hs, None, tq, PREV_ROWS + tq), lambda h, qb: (h, jnp.minimum(qb, nkb - 1), 0, 0))],
        out_specs=pl.BlockSpec((tq, hs * dh), lambda h, qb: (qb, h)),
        out_shape=jax.ShapeDtypeStruct((t, nh * dh), BF16),
        compiler_params=_cparams(("parallel", "parallel"), 32),
        name="band_attn_prompt",
    )(qkv, *([qkv] * (2 * nkb)), bias)


def _attn_sample_kernel(q_ref, kc_ref, kn_ref, vc_ref, vn_ref, bias_ref, o_ref, *, dh):
    for hh in range(ATT_HEADS_PER_STEP):
        cols = slice(hh * dh, (hh + 1) * dh)
        k = jnp.concatenate([kc_ref[:, :, cols], kn_ref[:, :, cols]], axis=1).astype(BF16)
        v = jnp.concatenate([vc_ref[:, :, cols], vn_ref[:, :, cols]], axis=1).astype(BF16)
        s = jnp.einsum('bqd,bkd->bqk', q_ref[:, :, cols].astype(BF16), k, preferred_element_type=F32)
        p, l = _softmax_parts(s + bias_ref[hh][None])
        o = jnp.einsum('bqk,bkd->bqd', p, v, preferred_element_type=F32) / l
        o_ref[:, :, cols] = o.astype(o_ref.dtype)


def _attn_sample(q, kc, kn, vc, vn, bias, *, nh, dh):
    b, t, d = q.shape
    nr = kc.shape[1]
    hs = ATT_HEADS_PER_STEP
    new = pl.BlockSpec((b, t, hs * dh), lambda h: (0, 0, h))
    old = pl.BlockSpec((b, nr, hs * dh), lambda h: (0, 0, h))
    return pl.pallas_call(
        functools.partial(_attn_sample_kernel, dh=dh),
        grid=(nh // hs,),
        in_specs=[new, old, new, old, new, pl.BlockSpec((hs, t, nr + t), lambda h: (h, 0, 0))],
        out_specs=new,
        out_shape=jax.ShapeDtypeStruct((b, t, d), BF16),
        compiler_params=_cparams(("parallel",), 48),
        name="band_attn_sample",
    )(q, kc, kn, vc, vn, bias)


def _run_group(x3, mods, pool_hist, shift_prev, wkv0, cache, pos0, p):
    bsz, t, d = x3.shape
    rows = bsz * t
    dp = pool_hist.shape[-1]
    ds = shift_prev.shape[-1]
    nh_att = p['rel_bias'].shape[0]
    dh = d // nh_att
    x = x3.reshape(rows, d)

    sh1, sc1, gt1, sh2, sc2, gt2 = mods[0]
    gn = p['g_norm'][0]
    proj = _norm_mm(x, gn[0:1], sc1, sh1, p['w_in'], tm=512, tn_cap=2304)
    assert t >= POOL_PAD - 1
    hist_pad = jnp.pad(pool_hist, ((0, 0), (POOL_PAD - pool_hist.shape[1], 0), (0, 0)))
    pool_out = _pool_mix(proj, ds, hist_pad, p['w_pool'], p['pool_scale'], bsz=bsz, t=t, pos0=pos0)
    zfirst = jnp.pad(shift_prev[:, None, :], ((0, 0), (SUBLANES - 1, 0), (0, 0)))
    r, lw, kmod, v, kk, b, g = _rwkv_prep(proj, zfirst, p['mu_shift'], p['w_lora'], p['w0_decay'], p['a0_iclr'],
                                          p['k_k'], p['k_a'], p['seg_red'], p['seg_exp'], bsz=bsz, t=t)
    y, wkv_new = _wkv(r, lw, kmod, v, kk, b, wkv0, t=t)
    yg = _rwkv_post(y, r, kmod, v, g, p['lnx_w'], p['lnx_b'], p['r_k'], p['seg_red'], p['seg_exp'])
    x = _mm_norm_res([pool_out, yg], p['w_out0'], gn[1:2], gt1, x)
    a = _norm_mm(x, gn[2:3], sc2, sh2, p['w_ff1'][0], relu2=True, out_dtype=BF16)
    x = _mm_norm_res([a], p['w_ff2'][0], gn[3:4], gt2, x)
    tail = proj.reshape(bsz, t, -1)[:, t - (POOL_PAD - 1):]
    pool_new = tail[:, :, ds:]
    shift_new = tail[:, POOL_PAD - 2, :ds]

    sh1, sc1, gt1, sh2, sc2, gt2 = mods[1]
    gn = p['g_norm'][1]
    if cache is None:
        assert bsz == 1 and pos0 == 0
        qkv = _norm_mm(x, gn[0:1], sc1, sh1, p['w_qkv1'], out_dtype=BF16)
        o = _attn_prompt(qkv, p['bias_tile'], nh=nh_att, dh=dh)
        keep = min(PREV_ROWS, t)
        kv = _norm_mm(x[t - keep:], gn[0:1], sc1, sh1, p['w_qkv1'][:, d:])
        k_new = kv[:, :d].reshape(1, keep, nh_att, dh)
        v_new = kv[:, d:].reshape(1, keep, nh_att, dh)
    else:
        cache_k, cache_v = cache
        nr = cache_k.shape[1]
        assert nr == PREV_ROWS and pos0 % CHUNK == 0 and pos0 >= nr and t <= CHUNK
        qkv3 = _norm_mm(x, gn[0:1], sc1, sh1, p['w_qkv1']).reshape(bsz, t, 3 * d)
        k_new, v_new = qkv3[:, :, d:2 * d], qkv3[:, :, 2 * d:]
        bias = p['bias_tile'][:, -1, :t, :nr + t]
        o = _attn_sample(qkv3[:, :, :d], cache_k.reshape(bsz, nr, d), k_new, cache_v.reshape(bsz, nr, d), v_new,
                         bias, nh=nh_att, dh=dh).reshape(rows, d)
        k_new = k_new.reshape(bsz, t, nh_att, dh)
        v_new = v_new.reshape(bsz, t, nh_att, dh)
    x = _mm_norm_res([o], p['w_out1'], gn[1:2], gt1, x)
    a = _norm_mm(x, gn[2:3], sc2, sh2, p['w_ff1'][1], relu2=True, out_dtype=BF16)
    x = _mm_norm_res([a], p['w_ff2'][1], gn[3:4], gt2, x)
    return x.reshape(bsz, t, d), pool_new, shift_new, wkv_new, k_new, v_new


def kernel(x_prompt, x_sample, c_prompt, c_sample, state_l0_pool, state_l0_shift, state_l0_wkv, cache_l1_k, cache_l1_v, w_ada, b_ada, g_norm, w_in0, w_pool, pool_scale, mu_shift, w0_decay, w2_decay, a0_iclr, a2_iclr, g2_gate, k_k, k_a, r_k, lnx_w, lnx_b, w_out0, w_qkv1, rel_bias, w_out1, w_ff1, w_ff2):
    bp, tp, d = x_prompt.shape
    bs, ts, _ = x_sample.shape
    depth = w_ada.shape[0]
    dp = state_l0_pool.shape[-1]
    dr = d - dp
    nh = dr // RWKV_HEAD
    nh_att = rel_bias.shape[0]
    n_dec, n_iclr, n_gate = w2_decay.shape[0], a2_iclr.shape[0], g2_gate.shape[0]
    row = lambda a: a.reshape(1, -1)

    n_c = bp + bs
    c_all = jnp.pad(jnp.concatenate([c_prompt, c_sample], axis=0), ((0, -n_c % SUBLANES), (0, 0)))
    mod = _ada_mod(c_all, w_ada, b_ada)
    mods_p = [[mod[l, 0:bp, i * d:(i + 1) * d] for i in range(6)] for l in range(depth)]
    mods_s = [[jnp.repeat(mod[l, bp:n_c, i * d:(i + 1) * d], ts, axis=0) for i in range(6)] for l in range(depth)]

    seg = (lax.broadcasted_iota(jnp.int32, (dr, LANES), 0) // RWKV_HEAD
           == lax.broadcasted_iota(jnp.int32, (dr, LANES), 1)).astype(BF16)
    w_lora = jnp.zeros((3, n_dec + n_iclr + n_gate, dr), F32)
    w_lora = w_lora.at[0, :n_dec].set(w2_decay).at[1, n_dec:n_dec + n_iclr].set(a2_iclr)
    w_lora = w_lora.at[2, n_dec + n_iclr:].set(g2_gate)
    q_scale = jnp.concatenate([jnp.full((d,), (d // nh_att) ** -0.5, F32), jnp.ones((2 * d,), F32)])
    p = {
        'g_norm': g_norm,
        'w_in': jnp.concatenate([w_in0[:, dp:], w_in0[:, :dp]], axis=1).astype(BF16),
        'w_pool': w_pool.astype(BF16), 'pool_scale': row(pool_scale), 'mu_shift': row(mu_shift),
        'w_lora': w_lora.astype(BF16), 'w0_decay': row(w0_decay), 'a0_iclr': row(a0_iclr),
        'k_k': row(k_k), 'k_a': row(k_a), 'r_k': row(r_k), 'lnx_w': row(lnx_w), 'lnx_b': row(lnx_b),
        'seg_red': seg, 'seg_exp': seg.T,
        'w_out0': w_out0.astype(BF16), 'w_qkv1': (w_qkv1 * q_scale).astype(BF16), 'w_out1': w_out1.astype(BF16),
        'w_ff1': w_ff1.astype(BF16), 'w_ff2': w_ff2.astype(BF16),
        'rel_bias': rel_bias, 'bias_tile': _bias_tile(rel_bias, tq=ATT_TQ),
    }

    y_p, pool_p, shift_p, wkv_p, k_p, v_p = _run_group(
        x_prompt, mods_p, jnp.zeros((bp, POOL_PAD - 1, dp), F32), jnp.zeros((bp, w_in0.shape[1] - dp), F32),
        jnp.zeros((bp, nh, RWKV_HEAD, RWKV_HEAD), F32), None, 0, p)
    y_s, pool_s, shift_s, wkv_s, k_s, v_s = _run_group(
        x_sample, mods_s, state_l0_pool, state_l0_shift, state_l0_wkv, (cache_l1_k, cache_l1_v), PAST_LEN, p)
    return (y_p, y_s, pool_p, pool_s, shift_p, shift_s, wkv_p, wkv_s, k_p, v_p, k_s, v_s)
```

```python
import functools

import jax
import jax.numpy as jnp
from jax import lax
from jax.experimental import pallas as pl
from jax.experimental.pallas import tpu as pltpu

F32 = jnp.float32
BF16 = jnp.bfloat16

NORM_EPS = 1e-6
LNX_EPS = 64e-5
CHUNK = 64
POOL_WINDOWS = (2, 4, 8, 16)
POOL_PAD = 16
RWKV_HEAD = 64
DECAY_SCALE = 0.6065306597126334
WKV_CHUNK = 128
N_PREV_CHUNKS = 8
PREV_ROWS = N_PREV_CHUNKS * CHUNK
REL_CLIP = 2 * CHUNK
PAST_LEN = 2048
NEG_INF = -1e30
V7X_VMEM_BYTES = 64 * 1024 * 1024
LANES = 128
SUBLANES = 8
ATT_TQ = 256
ATT_HEADS_PER_STEP = 4


def _cparams(semantics, vmem_mb):
    assert vmem_mb * 1024 * 1024 <= V7X_VMEM_BYTES
    return pltpu.CompilerParams(dimension_semantics=semantics, vmem_limit_bytes=vmem_mb * 1024 * 1024)


def _col_tile(n, cap):
    best = None
    for t in range(LANES, min(n, cap) + 1, LANES):
        if n % t == 0:
            best = t
    assert best is not None
    return best


def _dot(a, b):
    return jnp.dot(a.astype(BF16), b.astype(BF16), preferred_element_type=F32)


def _dot_nt(a, b):
    return lax.dot_general(a.astype(BF16), b.astype(BF16), (((1,), (1,)), ((), ())), preferred_element_type=F32)


def _split2(x):
    hi = x.astype(BF16)
    return hi, (x - hi.astype(F32)).astype(BF16)


def _split3(x):
    hi = x.astype(BF16)
    r1 = x - hi.astype(F32)
    mid = r1.astype(BF16)
    lo = (r1 - mid.astype(F32)).astype(BF16)
    return hi, mid, lo


def _segsum(x, red, exp):
    hi, lo = _split2(jnp.dot(x.astype(BF16), red, preferred_element_type=F32))
    return jnp.dot(hi, exp, preferred_element_type=F32) + jnp.dot(lo, exp, preferred_element_type=F32)


def _sigmoid(x):
    return 1.0 / (1.0 + jnp.exp(-x))


def _ada_kernel(c_ref, w_ref, b_ref, o_ref):
    c = c_ref[...]
    o_ref[...] = _dot(c * _sigmoid(c), w_ref[...]) + b_ref[...]


def _ada_mod(c_all, w_ada, b_ada, *, tn=1024):
    depth, d, n = w_ada.shape
    rows = c_all.shape[0]
    return pl.pallas_call(
        _ada_kernel,
        grid=(depth, n // tn),
        in_specs=[pl.BlockSpec((rows, d), lambda l, j: (0, 0)),
                  pl.BlockSpec((None, d, tn), lambda l, j: (l, 0, j)),
                  pl.BlockSpec((None, 1, tn), lambda l, j: (l, 0, j))],
        out_specs=pl.BlockSpec((None, rows, tn), lambda l, j: (l, 0, j)),
        out_shape=jax.ShapeDtypeStruct((depth, rows, n), F32),
        compiler_params=_cparams(("parallel", "parallel"), 40),
        name="ada_mod",
    )(c_all, w_ada, b_ada.reshape(depth, 1, n))


def _norm_mm_kernel(x_ref, g_ref, sc_ref, sh_ref, w_ref, o_ref, h_ref, *, relu2):
    @pl.when(pl.program_id(1) == 0)
    def _():
        x = x_ref[...]
        y = x * lax.rsqrt(jnp.mean(x * x, axis=-1, keepdims=True) + NORM_EPS) * g_ref[...]
        h_ref[...] = (y * (1.0 + sc_ref[...]) + sh_ref[...]).astype(h_ref.dtype)

    acc = jnp.dot(h_ref[...], w_ref[...], preferred_element_type=F32)
    if relu2:
        acc = jnp.square(jnp.maximum(acc, 0.0))
    o_ref[...] = acc.astype(o_ref.dtype)


def _norm_mm(x, g, sc, sh, w, *, layer=None, col0=0, relu2=False, out_dtype=F32, tm=1024, tn_cap=1024):
    rows, d = x.shape
    n = w.shape[-1] - col0
    tm = min(rows, tm)
    tn = _col_tile(n, tn_cap if rows > tm else 2 * tn_cap)
    assert col0 % tn == 0
    cb = col0 // tn
    mod_spec = (pl.BlockSpec((tm, d), lambda i, j: (i, 0)) if sc.shape[0] == rows
                else pl.BlockSpec((1, d), lambda i, j: (0, 0)))
    w_spec = (pl.BlockSpec((d, tn), lambda i, j: (0, cb + j)) if layer is None
              else pl.BlockSpec((None, d, tn), lambda i, j: (layer, 0, cb + j)))
    return pl.pallas_call(
        functools.partial(_norm_mm_kernel, relu2=relu2),
        grid=(rows // tm, n // tn),
        in_specs=[pl.BlockSpec((tm, d), lambda i, j: (i, 0)),
                  pl.BlockSpec((1, d), lambda i, j: (0, 0)),
                  mod_spec, mod_spec, w_spec],
        out_specs=pl.BlockSpec((tm, tn), lambda i, j: (i, j)),
        out_shape=jax.ShapeDtypeStruct((rows, n), out_dtype),
        scratch_shapes=[pltpu.VMEM((tm, d), BF16)],
        compiler_params=_cparams(("parallel", "arbitrary"), 56),
        name="norm_mm",
    )(x, g, sc, sh, w)


def _norm_res_epilogue(o, g_ref, gate_ref, x_ref, o_ref):
    y = o * lax.rsqrt(jnp.mean(o * o, axis=-1, keepdims=True) + NORM_EPS) * g_ref[...]
    o_ref[...] = x_ref[...] + gate_ref[...] * y


def _mm_norm_res_kernel(a_ref, w_ref, g_ref, gate_ref, x_ref, o_ref, acc_ref):
    k = pl.program_id(1)

    @pl.when(k == 0)
    def _():
        acc_ref[...] = jnp.zeros_like(acc_ref)

    acc_ref[...] += jnp.dot(a_ref[...], w_ref[...], preferred_element_type=F32)

    @pl.when(k == pl.num_programs(1) - 1)
    def _():
        _norm_res_epilogue(acc_ref[...], g_ref, gate_ref, x_ref, o_ref)


def _mm_norm_res_1step_kernel(*refs, n_a):
    a_refs = refs[:n_a]
    w_ref, g_ref, gate_ref, x_ref, o_ref = refs[n_a:]
    ka = a_refs[0].shape[1]
    o = jnp.dot(a_refs[0][...], w_ref[0:ka, :], preferred_element_type=F32)
    for idx in range(1, n_a):
        o = o + jnp.dot(a_refs[idx][...], w_ref[idx * ka:(idx + 1) * ka, :], preferred_element_type=F32)
    _norm_res_epilogue(o, g_ref, gate_ref, x_ref, o_ref)


def _mm_norm_res(a_list, w, g, gate, x, *, layer=None, tk_cap=2048):
    rows, d = x.shape
    tm = min(rows, 512)
    ka = a_list[0].shape[1]
    n_a = len(a_list)
    k_all = n_a * ka
    tk = min(k_all, tk_cap)
    nk = k_all // tk
    assert nk == 1 or n_a == 1
    row_spec = pl.BlockSpec((tm, d), lambda i, k: (i, 0))
    gate_spec = row_spec if gate.shape[0] == rows else pl.BlockSpec((1, d), lambda i, k: (0, 0))
    w_spec = (pl.BlockSpec((tk, d), lambda i, k: (k, 0)) if layer is None
              else pl.BlockSpec((None, tk, d), lambda i, k: (layer, k, 0)))
    a_specs = [pl.BlockSpec((tm, ka if nk == 1 else tk), lambda i, k: (i, k)) for _ in a_list]
    return pl.pallas_call(
        functools.partial(_mm_norm_res_1step_kernel, n_a=n_a) if nk == 1 else _mm_norm_res_kernel,
        grid=(rows // tm, nk),
        in_specs=a_specs + [w_spec, pl.BlockSpec((1, d), lambda i, k: (0, 0)), gate_spec, row_spec],
        out_specs=row_spec,
        out_shape=jax.ShapeDtypeStruct((rows, d), F32),
        scratch_shapes=[] if nk == 1 else [pltpu.VMEM((tm, d), F32)],
        compiler_params=_cparams(("parallel", "arbitrary"), 56),
        name="mm_norm_res",
    )(*a_list, w, g, gate, x)


def _pool_kernel(*refs, tm, pos0):
    ng = len(POOL_WINDOWS)
    u_refs, prev_refs = refs[:ng], refs[ng:2 * ng]
    hist_ref, w_ref, ps_ref, o_ref = refs[2 * ng:]
    i = pl.program_id(1)
    gc = u_refs[0].shape[1]
    pos = (pos0 + i * tm + lax.broadcasted_iota(jnp.int32, (tm, gc), 0)).astype(F32)
    for gi, win in enumerate(POOL_WINDOWS):
        prev = jnp.where(i == 0, hist_ref[:, gi * gc:(gi + 1) * gc], prev_refs[gi][...])
        u = u_refs[gi][...]
        s, d = jnp.concatenate([prev, u], axis=0), 1
        while d < win:
            s = s + pltpu.roll(s, d, axis=0)
            d *= 2
        cnt = jnp.minimum(F32(win), pos + 1.0)
        dev = s[POOL_PAD:, :] / cnt - u
        o = _dot(dev, w_ref[gi]) * ps_ref[:, gi * gc:(gi + 1) * gc]
        o_ref[:, gi * gc:(gi + 1) * gc] = o.astype(o_ref.dtype)


def _pool_mix(p_all, col0, hist_pad, w_pool, pool_scale, *, bsz, t, pos0):
    dp = hist_pad.shape[-1]
    ng = len(POOL_WINDOWS)
    gc = dp // ng
    tm = min(t, 512)
    assert max(POOL_WINDOWS) <= POOL_PAD and tm % POOL_PAD == 0 and col0 % gc == 0
    nt, blk, cb = t // tm, tm // POOL_PAD, col0 // gc
    cur = [pl.BlockSpec((tm, gc), lambda b, i, gi=gi: (b * nt + i, cb + gi)) for gi in range(ng)]
    prev = [pl.BlockSpec((POOL_PAD, gc), lambda b, i, gi=gi: (b * nt * blk + jnp.maximum(i * blk - 1, 0), cb + gi))
            for gi in range(ng)]
    return pl.pallas_call(
        functools.partial(_pool_kernel, tm=tm, pos0=pos0),
        grid=(bsz, nt),
        in_specs=cur + prev + [pl.BlockSpec((None, POOL_PAD, dp), lambda b, i: (b, 0, 0)),
                               pl.BlockSpec(w_pool.shape, lambda b, i: (0, 0, 0)),
                               pl.BlockSpec((1, dp), lambda b, i: (0, 0))],
        out_specs=pl.BlockSpec((tm, dp), lambda b, i: (b * nt + i, 0)),
        out_shape=jax.ShapeDtypeStruct((bsz * t, dp), BF16),
        compiler_params=_cparams(("parallel", "parallel"), 32),
        name="pool_mix",
    )(*([p_all] * (2 * ng)), hist_pad, w_pool, pool_scale)


def _prep_kernel(z_ref, zprev_ref, zfirst_ref, mu_ref, wl_ref, w0_ref, a0_ref, kk_ref, ka_ref, red_ref, exp_ref,
                 r_out, lw_out, k_out, v_out, kk_out, b_out, g_out, *, dr):
    i = pl.program_id(1)
    z = z_ref[...]
    last = SUBLANES - 1
    prev_row = jnp.where(i == 0, zfirst_ref[last:, :], zprev_ref[last:, :])
    rows = lax.broadcasted_iota(jnp.int32, z.shape, 0)
    z_prev = jnp.where(rows == 0, prev_row, pltpu.roll(z, 1, axis=0))
    zs = z + (z_prev - z) * mu_ref[...]
    r, k, v, xl = zs[:, :dr], zs[:, dr:2 * dr], zs[:, 2 * dr:3 * dr], zs[:, 3 * dr:]
    log_decay = -DECAY_SCALE * _sigmoid(w0_ref[...] + _dot(jnp.tanh(xl), wl_ref[0]))
    a = _sigmoid(a0_ref[...] + _dot(xl, wl_ref[1]))
    g = _dot(_sigmoid(xl), wl_ref[2])
    kk = k * kk_ref[...]
    kk = kk / jnp.maximum(jnp.sqrt(_segsum(kk * kk, red_ref[...], exp_ref[...])), 1e-12)
    r_out[...] = r
    lw_out[...] = log_decay
    k_out[...] = k * (1.0 + (a - 1.0) * ka_ref[...])
    v_out[...] = v
    kk_out[...] = kk
    b_out[...] = kk * a
    g_out[...] = g


def _rwkv_prep(p_all, zfirst, mu, wl, w0, a0, k_k, k_a, red, exp, *, bsz, t):
    ds = zfirst.shape[-1]
    dr = w0.shape[1]
    tm = min(t, 256)
    nt, blk = t // tm, tm // SUBLANES
    vec = pl.BlockSpec((1, dr), lambda b, i: (0, 0))
    tile = pl.BlockSpec((tm, dr), lambda b, i: (b * nt + i, 0))
    out = jax.ShapeDtypeStruct((bsz * t, dr), F32)
    return pl.pallas_call(
        functools.partial(_prep_kernel, dr=dr),
        grid=(bsz, nt),
        in_specs=[pl.BlockSpec((tm, ds), lambda b, i: (b * nt + i, 0)),
                  pl.BlockSpec((SUBLANES, ds), lambda b, i: (b * nt * blk + jnp.maximum(i * blk - 1, 0), 0)),
                  pl.BlockSpec((None, SUBLANES, ds), lambda b, i: (b, 0, 0)),
                  pl.BlockSpec((1, ds), lambda b, i: (0, 0)),
                  pl.BlockSpec(wl.shape, lambda b, i: (0, 0, 0)),
                  vec, vec, vec, vec,
                  pl.BlockSpec(red.shape, lambda b, i: (0, 0)),
                  pl.BlockSpec(exp.shape, lambda b, i: (0, 0))],
        out_specs=[tile] * 7,
        out_shape=[out] * 7,
        compiler_params=_cparams(("parallel", "parallel"), 48),
        name="rwkv_prep",
    )(p_all, p_all, zfirst, mu, wl, w0, a0, k_k, k_a, red, exp)


def _wkv_kernel(r_ref, lw_ref, k_ref, v_ref, kk_ref, b_ref, s0_ref, y_ref, sout_ref, s_ref, *, nh, hd):
    c = pl.program_id(1)

    @pl.when(c == 0)
    def _():
        s_ref[...] = s0_ref[...]

    lw = lw_ref[...]
    n_tok = lw.shape[0]
    ti = lax.broadcasted_iota(jnp.int32, (n_tok, n_tok), 0)
    tj = lax.broadcasted_iota(jnp.int32, (n_tok, n_tok), 1)
    tri = (ti >= tj).astype(BF16)
    hi, mid, lo = _split3(lw)
    cum = (jnp.dot(tri, hi, preferred_element_type=F32) + jnp.dot(tri, mid, preferred_element_type=F32)
           + jnp.dot(tri, lo, preferred_element_type=F32))
    tot = cum[n_tok - 1:n_tok, :]
    e_neg = jnp.exp(-cum)
    e_rem = jnp.exp(tot - cum)
    kk, b, k = kk_ref[...], b_ref[...], k_ref[...]

    def heads(x):
        return jnp.stack([x[:, h * hd:(h + 1) * hd] for h in range(nh)])

    aw = heads(-kk * jnp.exp(cum - lw))
    rw = heads(r_ref[...] * jnp.exp(cum))
    bw = heads(b * e_neg)
    kw = heads(k * e_neg)
    bh = heads(b * e_rem)
    kh = heads(k * e_rem)
    v = heads(v_ref[...])
    w_tot = heads(jnp.exp(tot))

    def bmm(spec, x, y):
        return jnp.einsum(spec, x.astype(BF16), y.astype(BF16), preferred_element_type=F32)

    strict = (ti > tj)[None]
    incl = (ti >= tj)[None]
    aw_rw = jnp.concatenate([aw, rw], axis=1)
    on_b = bmm('hlk,hmk->hlm', aw_rw, bw)
    on_k = bmm('hlk,hmk->hlm', aw_rw, kw)
    a_ab = jnp.where(strict, on_b[:, :n_tok], 0.0)
    a_ak = jnp.where(strict, on_k[:, :n_tok], 0.0)
    a_rb_rk = jnp.concatenate([jnp.where(incl, on_b[:, n_tok:], 0.0), jnp.where(incl, on_k[:, n_tok:], 0.0)], axis=2)
    eye = jnp.where(ti == tj, 1.0, 0.0)[None]
    t_inv, blk = eye, 1
    while blk < n_tok:
        sh = blk.bit_length() - 1
        pair = ((jnp.right_shift(ti, sh + 1) == jnp.right_shift(tj, sh + 1))
                & ((jnp.right_shift(ti, sh) & 1) == 1) & ((jnp.right_shift(tj, sh) & 1) == 0))
        m = jnp.where(pair[None], a_ab, 0.0)
        if blk == 1:
            t_inv = t_inv + m
        else:
            t_inv = t_inv + bmm('hlm,hmv->hlv', t_inv, bmm('hlm,hmv->hlv', m, t_inv))
        blk *= 2
    rhs = jnp.concatenate([aw, bmm('hlm,hmv->hlv', a_ak, v)], axis=2)
    x = rhs + bmm('hlm,hmv->hlv', t_inv - eye, rhs)
    a_bar, u_c = x[:, :, :hd], x[:, :, hd:]

    s = s_ref[...]
    on_s = bmm('hlk,hvk->hlv', jnp.concatenate([a_bar, rw], axis=1), s)
    uv = jnp.concatenate([on_s[:, :n_tok] + u_c, v], axis=1)
    y = on_s[:, n_tok:] + bmm('hlm,hmv->hlv', a_rb_rk, uv)
    s_new = s * w_tot + bmm('hlv,hlk->hvk', uv, jnp.concatenate([bh, kh], axis=1))
    s_ref[...] = s_new
    y_ref[...] = jnp.concatenate([y[h] for h in range(nh)], axis=1)

    @pl.when(c == pl.num_programs(1) - 1)
    def _():
        sout_ref[...] = s_new


def _wkv(r, lw, k, v, kk, b, s0, *, t):
    dr = r.shape[1]
    bsz, nh, hd = s0.shape[0], s0.shape[1], s0.shape[2]
    n_tok = min(t, WKV_CHUNK)
    nc = t // n_tok
    tile = pl.BlockSpec((n_tok, dr), lambda bi, c: (bi * nc + c, 0))
    st = pl.BlockSpec((None, nh, hd, hd), lambda bi, c: (bi, 0, 0, 0))
    return pl.pallas_call(
        functools.partial(_wkv_kernel, nh=nh, hd=hd),
        grid=(bsz, nc),
        in_specs=[tile] * 6 + [st],
        out_specs=[tile, st],
        out_shape=[jax.ShapeDtypeStruct(r.shape, F32), jax.ShapeDtypeStruct(s0.shape, F32)],
        scratch_shapes=[pltpu.VMEM((nh, hd, hd), F32)],
        compiler_params=_cparams(("parallel", "arbitrary"), 48),
        name="wkv7",
    )(r, lw, k, v, kk, b, s0)


def _post_kernel(y_ref, r_ref, k_ref, v_ref, g_ref, lw_ref, lb_ref, rk_ref, red_ref, exp_ref, o_ref, *, hd):
    red, exp = red_ref[...], exp_ref[...]
    y = y_ref[...]
    mu = _segsum(y, red, exp) * (1.0 / hd)
    yc = y - mu
    var = _segsum(yc * yc, red, exp) * (1.0 / hd)
    yn = yc * lax.rsqrt(var + LNX_EPS) * lw_ref[...] + lb_ref[...]
    bonus = _segsum(r_ref[...] * k_ref[...] * rk_ref[...], red, exp) * v_ref[...]
    o_ref[...] = ((yn + bonus) * g_ref[...]).astype(o_ref.dtype)


def _rwkv_post(y, r, k, v, g, lnx_w, lnx_b, r_k, red, exp):
    rows, dr = y.shape
    tm = min(rows, 512)
    tile = pl.BlockSpec((tm, dr), lambda i: (i, 0))
    vec = pl.BlockSpec((1, dr), lambda i: (0, 0))
    return pl.pallas_call(
        functools.partial(_post_kernel, hd=RWKV_HEAD),
        grid=(rows // tm,),
        in_specs=[tile] * 5 + [vec] * 3 + [pl.BlockSpec(red.shape, lambda i: (0, 0)),
                                           pl.BlockSpec(exp.shape, lambda i: (0, 0))],
        out_specs=tile,
        out_shape=jax.ShapeDtypeStruct((rows, dr), BF16),
        compiler_params=_cparams(("parallel",), 48),
        name="rwkv_post",
    )(y, r, k, v, g, lnx_w, lnx_b, r_k, red, exp)


def _bias_kernel(rb_ref, o_ref, tv_ref, *, tq, nk, width):
    h = pl.program_id(0)
    nrel = rb_ref.shape[1]
    m = lax.broadcasted_iota(jnp.int32, (nrel, width), 1)
    r = lax.broadcasted_iota(jnp.int32, (nrel, width), 0)
    m = jnp.where(m >= nk, m - width, m)
    sel = (jnp.clip(PREV_ROWS - m, -REL_CLIP, REL_CLIP) + REL_CLIP == r).astype(F32)
    tv_ref[...] = jnp.dot(rb_ref[...], sel, precision=lax.Precision.HIGHEST, preferred_element_type=F32)
    row = jnp.broadcast_to(tv_ref[pl.ds(h, 1), :], (tq, width))
    t = pltpu.roll(row, 0, axis=1, stride=1, stride_axis=0)[:, :nk]
    shift = CHUNK.bit_length() - 1
    col = lax.broadcasted_iota(jnp.int32, (tq, nk), 1)
    ci = lax.shift_right_logical(lax.broadcasted_iota(jnp.int32, (tq, nk), 0), shift)
    cj = lax.shift_right_logical(col, shift)
    band = jnp.where((cj >= ci) & (cj <= ci + N_PREV_CHUNKS), t, NEG_INF)
    for var in range(o_ref.shape[0]):
        o_ref[var] = jnp.where(col + var * tq - PREV_ROWS >= 0, band, NEG_INF)


def _bias_tile(rel_bias, *, tq):
    nh, nrel = rel_bias.shape
    nk = PREV_ROWS + tq
    n_var = PREV_ROWS // tq + 1
    width = -(-(nk + tq) // LANES) * LANES
    nrel_pad = -(-nrel // LANES) * LANES
    rb = jnp.pad(rel_bias, ((0, 0), (0, nrel_pad - nrel)))
    return pl.pallas_call(
        functools.partial(_bias_kernel, tq=tq, nk=nk, width=width),
        grid=(nh,),
        in_specs=[pl.BlockSpec((nh, nrel_pad), lambda h: (0, 0))],
        out_specs=pl.BlockSpec((None, n_var, tq, nk), lambda h: (h, 0, 0, 0)),
        out_shape=jax.ShapeDtypeStruct((nh, n_var, tq, nk), F32),
        scratch_shapes=[pltpu.VMEM((nh, width), F32)],
        compiler_params=_cparams(("arbitrary",), 32),
        name="band_bias",
    )(rb)


def _softmax_parts(s):
    p = jnp.exp(s - jnp.max(s, axis=-1, keepdims=True))
    return p.astype(BF16), jnp.sum(p, axis=-1, keepdims=True)


def _attn_prompt_kernel(*refs, nkb, dh):
    q_ref = refs[0]
    k_refs, v_refs = refs[1:1 + nkb], refs[1 + nkb:1 + 2 * nkb]
    bias_ref, o_ref = refs[1 + 2 * nkb:]
    for hh in range(ATT_HEADS_PER_STEP):
        cols = slice(hh * dh, (hh + 1) * dh)
        k = jnp.concatenate([kr[:, cols] for kr in k_refs], axis=0)
        v = jnp.concatenate([vr[:, cols] for vr in v_refs], axis=0)
        p, l = _softmax_parts(_dot_nt(q_ref[:, cols], k) + bias_ref[hh])
        o_ref[:, cols] = (jnp.dot(p, v, preferred_element_type=F32) / l).astype(o_ref.dtype)


def _attn_prompt(qkv, bias, *, nh, dh):
    t = qkv.shape[0]
    tq, hs = ATT_TQ, ATT_HEADS_PER_STEP
    assert tq % CHUNK == 0 and PREV_ROWS % tq == 0 and nh % hs == 0
    nkb = PREV_ROWS // tq + 1
    nhp = nh // hs

    def kv_spec(which, back):
        return pl.BlockSpec((tq, hs * dh), lambda h, qb: (jnp.maximum(qb - back, 0), which * nhp + h))

    return pl.pallas_call(
        functools.partial(_attn_prompt_kernel, nkb=nkb, dh=dh),
        grid=(nhp, t // tq),
        in_specs=[pl.BlockSpec((tq, hs * dh), lambda h, qb: (qb, h))]
                 + [kv_spec(1, nkb - 1 - j) for j in range(nkb)]
                 + [kv_spec(2, nkb - 1 - j) for j in range(nkb)]
                 + [pl.BlockSpec((hs, None, tq, PREV_ROWS + tq), lambda h, qb: (h, jnp.minimum(qb, nkb - 1), 0, 0))],
        out_specs=pl.BlockSpec((tq, hs * dh), lambda h, qb: (qb, h)),
        out_shape=jax.ShapeDtypeStruct((t, nh * dh), BF16),
        compiler_params=_cparams(("parallel", "parallel"), 32),
        name="band_attn_prompt",
    )(qkv, *([qkv] * (2 * nkb)), bias)


def _attn_sample_kernel(q_ref, kc_ref, kn_ref, vc_ref, vn_ref, bias_ref, o_ref, *, dh):
    for hh in range(ATT_HEADS_PER_STEP):
        cols = slice(hh * dh, (hh + 1) * dh)
        k = jnp.concatenate([kc_ref[:, :, cols], kn_ref[:, :, cols]], axis=1).astype(BF16)
        v = jnp.concatenate([vc_ref[:, :, cols], vn_ref[:, :, cols]], axis=1).astype(BF16)
        s = jnp.einsum('bqd,bkd->bqk', q_ref[:, :, cols].astype(BF16), k, preferred_element_type=F32)
        p, l = _softmax_parts(s + bias_ref[hh][None])
        o = jnp.einsum('bqk,bkd->bqd', p, v, preferred_element_type=F32) / l
        o_ref[:, :, cols] = o.astype(o_ref.dtype)


def _attn_sample(q, kc, kn, vc, vn, bias, *, nh, dh):
    b, t, d = q.shape
    nr = kc.shape[1]
    hs = ATT_HEADS_PER_STEP
    new = pl.BlockSpec((b, t, hs * dh), lambda h: (0, 0, h))
    old = pl.BlockSpec((b, nr, hs * dh), lambda h: (0, 0, h))
    return pl.pallas_call(
        functools.partial(_attn_sample_kernel, dh=dh),
        grid=(nh // hs,),
        in_specs=[new, old, new, old, new, pl.BlockSpec((hs, t, nr + t), lambda h: (h, 0, 0))],
        out_specs=new,
        out_shape=jax.ShapeDtypeStruct((b, t, d), BF16),
        compiler_params=_cparams(("parallel",), 48),
        name="band_attn_sample",
    )(q, kc, kn, vc, vn, bias)


def _run_group(x3, mods, pool_hist, shift_prev, wkv0, cache, pos0, p):
    bsz, t, d = x3.shape
    rows = bsz * t
    dp = pool_hist.shape[-1]
    ds = shift_prev.shape[-1]
    nh_att = p['rel_bias'].shape[0]
    dh = d // nh_att
    x = x3.reshape(rows, d)

    sh1, sc1, gt1, sh2, sc2, gt2 = mods[0]
    gn = p['g_norm'][0]
    proj = _norm_mm(x, gn[0:1], sc1, sh1, p['w_in'], tm=512, tn_cap=2304)
    assert t >= POOL_PAD - 1
    hist_pad = jnp.pad(pool_hist, ((0, 0), (POOL_PAD - pool_hist.shape[1], 0), (0, 0)))
    pool_out = _pool_mix(proj, ds, hist_pad, p['w_pool'], p['pool_scale'], bsz=bsz, t=t, pos0=pos0)
    zfirst = jnp.pad(shift_prev[:, None, :], ((0, 0), (SUBLANES - 1, 0), (0, 0)))
    r, lw, kmod, v, kk, b, g = _rwkv_prep(proj, zfirst, p['mu_shift'], p['w_lora'], p['w0_decay'], p['a0_iclr'],
                                          p['k_k'], p['k_a'], p['seg_red'], p['seg_exp'], bsz=bsz, t=t)
    y, wkv_new = _wkv(r, lw, kmod, v, kk, b, wkv0, t=t)
    yg = _rwkv_post(y, r, kmod, v, g, p['lnx_w'], p['lnx_b'], p['r_k'], p['seg_red'], p['seg_exp'])
    x = _mm_norm_res([pool_out, yg], p['w_out0'], gn[1:2], gt1, x)
    a = _norm_mm(x, gn[2:3], sc2, sh2, p['w_ff1'], layer=0, relu2=True, out_dtype=BF16)
    x = _mm_norm_res([a], p['w_ff2'], gn[3:4], gt2, x, layer=0)
    tail = proj.reshape(bsz, t, -1)[:, t - (POOL_PAD - 1):]
    pool_new = tail[:, :, ds:]
    shift_new = tail[:, POOL_PAD - 2, :ds]

    sh1, sc1, gt1, sh2, sc2, gt2 = mods[1]
    gn = p['g_norm'][1]
    if cache is None:
        assert bsz == 1 and pos0 == 0
        qkv = _norm_mm(x, gn[0:1], sc1, sh1, p['w_qkv1'], out_dtype=BF16)
        o = _attn_prompt(qkv, p['bias_tile'], nh=nh_att, dh=dh)
        keep = min(PREV_ROWS, t)
        kv = _norm_mm(x[t - keep:], gn[0:1], sc1, sh1, p['w_qkv1'], col0=d)
        k_new = kv[:, :d].reshape(1, keep, nh_att, dh)
        v_new = kv[:, d:].reshape(1, keep, nh_att, dh)
    else:
        cache_k, cache_v = cache
        nr = cache_k.shape[1]
        assert nr == PREV_ROWS and pos0 % CHUNK == 0 and pos0 >= nr and t <= CHUNK
        qkv3 = _norm_mm(x, gn[0:1], sc1, sh1, p['w_qkv1']).reshape(bsz, t, 3 * d)
        k_new, v_new = qkv3[:, :, d:2 * d], qkv3[:, :, 2 * d:]
        bias = p['bias_tile'][:, -1, :t, :nr + t]
        o = _attn_sample(qkv3[:, :, :d], cache_k.reshape(bsz, nr, d), k_new, cache_v.reshape(bsz, nr, d), v_new,
                         bias, nh=nh_att, dh=dh).reshape(rows, d)
        k_new = k_new.reshape(bsz, t, nh_att, dh)
        v_new = v_new.reshape(bsz, t, nh_att, dh)
    x = _mm_norm_res([o], p['w_out1'], gn[1:2], gt1, x)
    a = _norm_mm(x, gn[2:3], sc2, sh2, p['w_ff1'], layer=1, relu2=True, out_dtype=BF16)
    x = _mm_norm_res([a], p['w_ff2'], gn[3:4], gt2, x, layer=1)
    return x.reshape(bsz, t, d), pool_new, shift_new, wkv_new, k_new, v_new


def kernel(x_prompt, x_sample, c_prompt, c_sample, state_l0_pool, state_l0_shift, state_l0_wkv, cache_l1_k, cache_l1_v, w_ada, b_ada, g_norm, w_in0, w_pool, pool_scale, mu_shift, w0_decay, w2_decay, a0_iclr, a2_iclr, g2_gate, k_k, k_a, r_k, lnx_w, lnx_b, w_out0, w_qkv1, rel_bias, w_out1, w_ff1, w_ff2):
    bp, tp, d = x_prompt.shape
    bs, ts, _ = x_sample.shape
    depth = w_ada.shape[0]
    dp = state_l0_pool.shape[-1]
    dr = d - dp
    nh = dr // RWKV_HEAD
    nh_att = rel_bias.shape[0]
    n_dec, n_iclr, n_gate = w2_decay.shape[0], a2_iclr.shape[0], g2_gate.shape[0]
    row = lambda a: a.reshape(1, -1)

    n_c = bp + bs
    c_all = jnp.pad(jnp.concatenate([c_prompt, c_sample], axis=0), ((0, -n_c % SUBLANES), (0, 0)))
    mod = _ada_mod(c_all, w_ada, b_ada)
    mods_p = [[mod[l, 0:bp, i * d:(i + 1) * d] for i in range(6)] for l in range(depth)]
    mods_s = [[jnp.repeat(mod[l, bp:n_c, i * d:(i + 1) * d], ts, axis=0) for i in range(6)] for l in range(depth)]

    seg = (lax.broadcasted_iota(jnp.int32, (dr, LANES), 0) // RWKV_HEAD
           == lax.broadcasted_iota(jnp.int32, (dr, LANES), 1)).astype(BF16)
    w_lora = jnp.zeros((3, n_dec + n_iclr + n_gate, dr), F32)
    w_lora = w_lora.at[0, :n_dec].set(w2_decay).at[1, n_dec:n_dec + n_iclr].set(a2_iclr)
    w_lora = w_lora.at[2, n_dec + n_iclr:].set(g2_gate)
    q_scale = jnp.concatenate([jnp.full((d,), (d // nh_att) ** -0.5, F32), jnp.ones((2 * d,), F32)])
    p = {
        'g_norm': g_norm,
        'w_in': jnp.concatenate([w_in0[:, dp:], w_in0[:, :dp]], axis=1).astype(BF16),
        'w_pool': w_pool.astype(BF16), 'pool_scale': row(pool_scale), 'mu_shift': row(mu_shift),
        'w_lora': w_lora.astype(BF16), 'w0_decay': row(w0_decay), 'a0_iclr': row(a0_iclr),
        'k_k': row(k_k), 'k_a': row(k_a), 'r_k': row(r_k), 'lnx_w': row(lnx_w), 'lnx_b': row(lnx_b),
        'seg_red': seg, 'seg_exp': seg.T,
        'w_out0': w_out0.astype(BF16), 'w_qkv1': (w_qkv1 * q_scale).astype(BF16), 'w_out1': w_out1.astype(BF16),
        'w_ff1': w_ff1.astype(BF16), 'w_ff2': w_ff2.astype(BF16),
        'rel_bias': rel_bias, 'bias_tile': _bias_tile(rel_bias, tq=ATT_TQ),
    }

    y_p, pool_p, shift_p, wkv_p, k_p, v_p = _run_group(
        x_prompt, mods_p, jnp.zeros((bp, POOL_PAD - 1, dp), F32), jnp.zeros((bp, w_in0.shape[1] - dp), F32),
        jnp.zeros((bp, nh, RWKV_HEAD, RWKV_HEAD), F32), None, 0, p)
    y_s, pool_s, shift_s, wkv_s, k_s, v_s = _run_group(
        x_sample, mods_s, state_l0_pool, state_l0_shift, state_l0_wkv, (cache_l1_k, cache_l1_v), PAST_LEN, p)
    return (y_p, y_s, pool_p, pool_s, shift_p, shift_s, wkv_p, wkv_s, k_p, v_p, k_s, v_s)
```

```python
import functools

import jax
import jax.numpy as jnp
from jax import lax
from jax.experimental import pallas as pl
from jax.experimental.pallas import tpu as pltpu

F32 = jnp.float32
BF16 = jnp.bfloat16

NORM_EPS = 1e-6
LNX_EPS = 64e-5
CHUNK = 64
POOL_WINDOWS = (2, 4, 8, 16)
POOL_PAD = 16
RWKV_HEAD = 64
DECAY_SCALE = 0.6065306597126334
WKV_CHUNK = 128
N_PREV_CHUNKS = 8
PREV_ROWS = N_PREV_CHUNKS * CHUNK
REL_CLIP = 2 * CHUNK
PAST_LEN = 2048
NEG_INF = -1e30
V7X_VMEM_BYTES = 64 * 1024 * 1024
LANES = 128
SUBLANES = 8
ATT_TQ = 256
ATT_HEADS_PER_STEP = 4


def _cparams(semantics, vmem_mb):
    assert vmem_mb * 1024 * 1024 <= V7X_VMEM_BYTES
    return pltpu.CompilerParams(dimension_semantics=semantics, vmem_limit_bytes=vmem_mb * 1024 * 1024)


def _col_tile(n, cap):
    best = None
    for t in range(LANES, min(n, cap) + 1, LANES):
        if n % t == 0:
            best = t
    assert best is not None
    return best


def _dot(a, b):
    return jnp.dot(a.astype(BF16), b.astype(BF16), preferred_element_type=F32)


def _dot_nt(a, b):
    return lax.dot_general(a.astype(BF16), b.astype(BF16), (((1,), (1,)), ((), ())), preferred_element_type=F32)


def _split2(x):
    hi = x.astype(BF16)
    return hi, (x - hi.astype(F32)).astype(BF16)


def _split3(x):
    hi = x.astype(BF16)
    r1 = x - hi.astype(F32)
    mid = r1.astype(BF16)
    lo = (r1 - mid.astype(F32)).astype(BF16)
    return hi, mid, lo


def _segsum(x, red, exp):
    hi, lo = _split2(jnp.dot(x.astype(BF16), red, preferred_element_type=F32))
    return jnp.dot(hi, exp, preferred_element_type=F32) + jnp.dot(lo, exp, preferred_element_type=F32)


def _sigmoid(x):
    return 1.0 / (1.0 + jnp.exp(-x))


def _ada_kernel(c_ref, w_ref, b_ref, o_ref):
    c = c_ref[...]
    o_ref[...] = _dot(c * _sigmoid(c), w_ref[...]) + b_ref[...]


def _ada_mod(c_all, w_ada, b_ada, *, tn=1024):
    depth, d, n = w_ada.shape
    rows = c_all.shape[0]
    return pl.pallas_call(
        _ada_kernel,
        grid=(depth, n // tn),
        in_specs=[pl.BlockSpec((rows, d), lambda l, j: (0, 0)),
                  pl.BlockSpec((None, d, tn), lambda l, j: (l, 0, j)),
                  pl.BlockSpec((None, 1, tn), lambda l, j: (l, 0, j))],
        out_specs=pl.BlockSpec((None, rows, tn), lambda l, j: (l, 0, j)),
        out_shape=jax.ShapeDtypeStruct((depth, rows, n), F32),
        compiler_params=_cparams(("parallel", "parallel"), 40),
        name="ada_mod",
    )(c_all, w_ada, b_ada.reshape(depth, 1, n))


def _norm_mm_kernel(x_ref, g_ref, sc_ref, sh_ref, w_ref, o_ref, h_ref, *, relu2):
    @pl.when(pl.program_id(1) == 0)
    def _():
        x = x_ref[...]
        y = x * lax.rsqrt(jnp.mean(x * x, axis=-1, keepdims=True) + NORM_EPS) * g_ref[...]
        h_ref[...] = (y * (1.0 + sc_ref[...]) + sh_ref[...]).astype(h_ref.dtype)

    acc = jnp.dot(h_ref[...], w_ref[...], preferred_element_type=F32)
    if relu2:
        acc = jnp.square(jnp.maximum(acc, 0.0))
    o_ref[...] = acc.astype(o_ref.dtype)


def _norm_mm(x, g, sc, sh, w, *, layer=None, col0=0, relu2=False, out_dtype=F32, tm=1024, tn_cap=1024):
    rows, d = x.shape
    n = w.shape[-1] - col0
    tm = min(rows, tm)
    tn = _col_tile(n, tn_cap if rows > tm else 2 * tn_cap)
    assert col0 % tn == 0
    cb = col0 // tn
    mod_spec = (pl.BlockSpec((tm, d), lambda i, j: (i, 0)) if sc.shape[0] == rows
                else pl.BlockSpec((1, d), lambda i, j: (0, 0)))
    w_spec = (pl.BlockSpec((d, tn), lambda i, j: (0, cb + j)) if layer is None
              else pl.BlockSpec((None, d, tn), lambda i, j: (layer, 0, cb + j)))
    return pl.pallas_call(
        functools.partial(_norm_mm_kernel, relu2=relu2),
        grid=(rows // tm, n // tn),
        in_specs=[pl.BlockSpec((tm, d), lambda i, j: (i, 0)),
                  pl.BlockSpec((1, d), lambda i, j: (0, 0)),
                  mod_spec, mod_spec, w_spec],
        out_specs=pl.BlockSpec((tm, tn), lambda i, j: (i, j)),
        out_shape=jax.ShapeDtypeStruct((rows, n), out_dtype),
        scratch_shapes=[pltpu.VMEM((tm, d), BF16)],
        compiler_params=_cparams(("parallel", "arbitrary"), 56),
        name="norm_mm",
    )(x, g, sc, sh, w)


def _norm_res_epilogue(o, g_ref, gate_ref, x_ref, o_ref):
    y = o * lax.rsqrt(jnp.mean(o * o, axis=-1, keepdims=True) + NORM_EPS) * g_ref[...]
    o_ref[...] = x_ref[...] + gate_ref[...] * y


def _mm_norm_res_kernel(a_ref, w_ref, g_ref, gate_ref, x_ref, o_ref, acc_ref):
    k = pl.program_id(1)

    @pl.when(k == 0)
    def _():
        acc_ref[...] = jnp.zeros_like(acc_ref)

    acc_ref[...] += jnp.dot(a_ref[...], w_ref[...], preferred_element_type=F32)

    @pl.when(k == pl.num_programs(1) - 1)
    def _():
        _norm_res_epilogue(acc_ref[...], g_ref, gate_ref, x_ref, o_ref)


def _mm_norm_res_1step_kernel(*refs, n_a):
    a_refs = refs[:n_a]
    w_ref, g_ref, gate_ref, x_ref, o_ref = refs[n_a:]
    ka = a_refs[0].shape[1]
    o = jnp.dot(a_refs[0][...], w_ref[0:ka, :], preferred_element_type=F32)
    for idx in range(1, n_a):
        o = o + jnp.dot(a_refs[idx][...], w_ref[idx * ka:(idx + 1) * ka, :], preferred_element_type=F32)
    _norm_res_epilogue(o, g_ref, gate_ref, x_ref, o_ref)


def _mm_norm_res(a_list, w, g, gate, x, *, layer=None, tk_cap=2048):
    rows, d = x.shape
    tm = min(rows, 512)
    ka = a_list[0].shape[1]
    n_a = len(a_list)
    k_all = n_a * ka
    tk = min(k_all, tk_cap)
    nk = k_all // tk
    assert nk == 1 or n_a == 1
    row_spec = pl.BlockSpec((tm, d), lambda i, k: (i, 0))
    gate_spec = row_spec if gate.shape[0] == rows else pl.BlockSpec((1, d), lambda i, k: (0, 0))
    w_spec = (pl.BlockSpec((tk, d), lambda i, k: (k, 0)) if layer is None
              else pl.BlockSpec((None, tk, d), lambda i, k: (layer, k, 0)))
    a_specs = [pl.BlockSpec((tm, ka if nk == 1 else tk), lambda i, k: (i, k)) for _ in a_list]
    return pl.pallas_call(
        functools.partial(_mm_norm_res_1step_kernel, n_a=n_a) if nk == 1 else _mm_norm_res_kernel,
        grid=(rows // tm, nk),
        in_specs=a_specs + [w_spec, pl.BlockSpec((1, d), lambda i, k: (0, 0)), gate_spec, row_spec],
        out_specs=row_spec,
        out_shape=jax.ShapeDtypeStruct((rows, d), F32),
        scratch_shapes=[] if nk == 1 else [pltpu.VMEM((tm, d), F32)],
        compiler_params=_cparams(("parallel", "arbitrary"), 56),
        name="mm_norm_res",
    )(*a_list, w, g, gate, x)


def _pool_kernel(*refs, tm, pos0):
    ng = len(POOL_WINDOWS)
    u_refs, prev_refs = refs[:ng], refs[ng:2 * ng]
    hist_ref, w_ref, ps_ref, o_ref = refs[2 * ng:]
    i = pl.program_id(1)
    gc = u_refs[0].shape[1]
    pos = (pos0 + i * tm + lax.broadcasted_iota(jnp.int32, (tm, gc), 0)).astype(F32)
    for gi, win in enumerate(POOL_WINDOWS):
        prev = jnp.where(i == 0, hist_ref[:, gi * gc:(gi + 1) * gc], prev_refs[gi][...])
        u = u_refs[gi][...]
        s, d = jnp.concatenate([prev, u], axis=0), 1
        while d < win:
            s = s + pltpu.roll(s, d, axis=0)
            d *= 2
        cnt = jnp.minimum(F32(win), pos + 1.0)
        dev = s[POOL_PAD:, :] / cnt - u
        o = _dot(dev, w_ref[gi]) * ps_ref[:, gi * gc:(gi + 1) * gc]
        o_ref[:, gi * gc:(gi + 1) * gc] = o.astype(o_ref.dtype)


def _pool_mix(p_all, col0, hist_pad, w_pool, pool_scale, *, bsz, t, pos0):
    dp = hist_pad.shape[-1]
    ng = len(POOL_WINDOWS)
    gc = dp // ng
    tm = min(t, 512)
    assert max(POOL_WINDOWS) <= POOL_PAD and tm % POOL_PAD == 0 and col0 % gc == 0
    nt, blk, cb = t // tm, tm // POOL_PAD, col0 // gc
    cur = [pl.BlockSpec((tm, gc), lambda b, i, gi=gi: (b * nt + i, cb + gi)) for gi in range(ng)]
    prev = [pl.BlockSpec((POOL_PAD, gc), lambda b, i, gi=gi: (b * nt * blk + jnp.maximum(i * blk - 1, 0), cb + gi))
            for gi in range(ng)]
    return pl.pallas_call(
        functools.partial(_pool_kernel, tm=tm, pos0=pos0),
        grid=(bsz, nt),
        in_specs=cur + prev + [pl.BlockSpec((None, POOL_PAD, dp), lambda b, i: (b, 0, 0)),
                               pl.BlockSpec(w_pool.shape, lambda b, i: (0, 0, 0)),
                               pl.BlockSpec((1, dp), lambda b, i: (0, 0))],
        out_specs=pl.BlockSpec((tm, dp), lambda b, i: (b * nt + i, 0)),
        out_shape=jax.ShapeDtypeStruct((bsz * t, dp), BF16),
        compiler_params=_cparams(("parallel", "parallel"), 32),
        name="pool_mix",
    )(*([p_all] * (2 * ng)), hist_pad, w_pool, pool_scale)


def _rwkv_token_prep(z, prev_row, mu_ref, wl_ref, w0_ref, a0_ref, kk_ref, ka_ref, red, exp, *, dr):
    rows = lax.broadcasted_iota(jnp.int32, z.shape, 0)
    z_prev = jnp.where(rows == 0, prev_row, pltpu.roll(z, 1, axis=0))
    zs = z + (z_prev - z) * mu_ref[...]
    r, k, v, xl = zs[:, :dr], zs[:, dr:2 * dr], zs[:, 2 * dr:3 * dr], zs[:, 3 * dr:]
    log_decay = -DECAY_SCALE * _sigmoid(w0_ref[...] + _dot(jnp.tanh(xl), wl_ref[0]))
    a = _sigmoid(a0_ref[...] + _dot(xl, wl_ref[1]))
    g = _dot(_sigmoid(xl), wl_ref[2])
    kk = k * kk_ref[...]
    kk = kk / jnp.maximum(jnp.sqrt(_segsum(kk * kk, red, exp)), 1e-12)
    return r, log_decay, k * (1.0 + (a - 1.0) * ka_ref[...]), v, kk, kk * a, g


def _wkv_chunk(r, lw, k, v, kk, b, s, *, nh, hd):
    n_tok = lw.shape[0]
    ti = lax.broadcasted_iota(jnp.int32, (n_tok, n_tok), 0)
    tj = lax.broadcasted_iota(jnp.int32, (n_tok, n_tok), 1)
    tri = (ti >= tj).astype(BF16)
    hi, mid, lo = _split3(lw)
    cum = (jnp.dot(tri, hi, preferred_element_type=F32) + jnp.dot(tri, mid, preferred_element_type=F32)
           + jnp.dot(tri, lo, preferred_element_type=F32))
    tot = cum[n_tok - 1:n_tok, :]
    e_neg = jnp.exp(-cum)
    e_rem = jnp.exp(tot - cum)

    def heads(x):
        return jnp.stack([x[:, h * hd:(h + 1) * hd] for h in range(nh)])

    aw = heads(-kk * jnp.exp(cum - lw))
    rw = heads(r * jnp.exp(cum))
    bw = heads(b * e_neg)
    kw = heads(k * e_neg)
    bh = heads(b * e_rem)
    kh = heads(k * e_rem)
    v = heads(v)
    w_tot = heads(jnp.exp(tot))

    def bmm(spec, x, y):
        return jnp.einsum(spec, x.astype(BF16), y.astype(BF16), preferred_element_type=F32)

    strict = (ti > tj)[None]
    incl = (ti >= tj)[None]
    aw_rw = jnp.concatenate([aw, rw], axis=1)
    on_b = bmm('hlk,hmk->hlm', aw_rw, bw)
    on_k = bmm('hlk,hmk->hlm', aw_rw, kw)
    a_ab = jnp.where(strict, on_b[:, :n_tok], 0.0)
    a_ak = jnp.where(strict, on_k[:, :n_tok], 0.0)
    a_rb_rk = jnp.concatenate([jnp.where(incl, on_b[:, n_tok:], 0.0), jnp.where(incl, on_k[:, n_tok:], 0.0)], axis=2)
    eye = jnp.where(ti == tj, 1.0, 0.0)[None]
    t_inv, blk = eye, 1
    while blk < n_tok:
        sh = blk.bit_length() - 1
        pair = ((jnp.right_shift(ti, sh + 1) == jnp.right_shift(tj, sh + 1))
                & ((jnp.right_shift(ti, sh) & 1) == 1) & ((jnp.right_shift(tj, sh) & 1) == 0))
        m = jnp.where(pair[None], a_ab, 0.0)
        if blk == 1:
            t_inv = t_inv + m
        else:
            t_inv = t_inv + bmm('hlm,hmv->hlv', t_inv, bmm('hlm,hmv->hlv', m, t_inv))
        blk *= 2
    rhs = jnp.concatenate([aw, bmm('hlm,hmv->hlv', a_ak, v)], axis=2)
    x = rhs + bmm('hlm,hmv->hlv', t_inv - eye, rhs)
    a_bar, u_c = x[:, :, :hd], x[:, :, hd:]

    on_s = bmm('hlk,hvk->hlv', jnp.concatenate([a_bar, rw], axis=1), s)
    uv = jnp.concatenate([on_s[:, :n_tok] + u_c, v], axis=1)
    y = on_s[:, n_tok:] + bmm('hlm,hmv->hlv', a_rb_rk, uv)
    s_new = s * w_tot + bmm('hlv,hlk->hvk', uv, jnp.concatenate([bh, kh], axis=1))
    return jnp.concatenate([y[h] for h in range(nh)], axis=1), s_new


def _rwkv_kernel(z_ref, zfirst_ref, mu_ref, wl_ref, w0_ref, a0_ref, kk_ref, ka_ref, lnw_ref, lnb_ref, rk_ref,
                 red_ref, exp_ref, s0_ref, o_ref, sout_ref, s_ref, zlast_ref, *, nh, hd):
    c = pl.program_id(1)

    @pl.when(c == 0)
    def _():
        s_ref[...] = s0_ref[...]
        zlast_ref[...] = zfirst_ref[...]

    red, exp = red_ref[...], exp_ref[...]
    z = z_ref[...]
    n_tok = z.shape[0]
    r, lw, k, v, kk, b, g = _rwkv_token_prep(z, zlast_ref[SUBLANES - 1:, :], mu_ref, wl_ref, w0_ref, a0_ref,
                                             kk_ref, ka_ref, red, exp, dr=nh * hd)
    zlast_ref[...] = z[n_tok - SUBLANES:, :]
    y, s_new = _wkv_chunk(r, lw, k, v, kk, b, s_ref[...], nh=nh, hd=hd)
    s_ref[...] = s_new

    mu = _segsum(y, red, exp) * (1.0 / hd)
    yc = y - mu
    var = _segsum(yc * yc, red, exp) * (1.0 / hd)
    yn = yc * lax.rsqrt(var + LNX_EPS) * lnw_ref[...] + lnb_ref[...]
    bonus = _segsum(r * k * rk_ref[...], red, exp) * v
    o_ref[...] = ((yn + bonus) * g).astype(o_ref.dtype)

    @pl.when(c == pl.num_programs(1) - 1)
    def _():
        sout_ref[...] = s_new


def _rwkv_mix(p_all, zfirst, s0, prm, *, t):
    ds = zfirst.shape[-1]
    bsz, nh, hd = s0.shape[0], s0.shape[1], s0.shape[2]
    dr = nh * hd
    n_tok = min(t, WKV_CHUNK)
    nc = t // n_tok
    assert n_tok % SUBLANES == 0
    vec = pl.BlockSpec((1, dr), lambda bi, c: (0, 0))
    tile = lambda w: pl.BlockSpec((n_tok, w), lambda bi, c: (bi * nc + c, 0))
    st = pl.BlockSpec((None, nh, hd, hd), lambda bi, c: (bi, 0, 0, 0))
    whole = lambda a: pl.BlockSpec(a.shape, lambda bi, c: (0,) * a.ndim)
    return pl.pallas_call(
        functools.partial(_rwkv_kernel, nh=nh, hd=hd),
        grid=(bsz, nc),
        in_specs=[tile(ds), pl.BlockSpec((None, SUBLANES, ds), lambda bi, c: (bi, 0, 0)),
                  pl.BlockSpec((1, ds), lambda bi, c: (0, 0)), whole(prm['w_lora'])] + [vec] * 7
                 + [whole(prm['seg_red']), whole(prm['seg_exp']), st],
        out_specs=[tile(dr), st],
        out_shape=[jax.ShapeDtypeStruct((bsz * t, dr), BF16), jax.ShapeDtypeStruct(s0.shape, F32)],
        scratch_shapes=[pltpu.VMEM((nh, hd, hd), F32), pltpu.VMEM((SUBLANES, ds), F32)],
        compiler_params=_cparams(("parallel", "arbitrary"), 48),
        name="rwkv_mix",
    )(p_all, zfirst, prm['mu_shift'], prm['w_lora'], prm['w0_decay'], prm['a0_iclr'], prm['k_k'], prm['k_a'],
      prm['lnx_w'], prm['lnx_b'], prm['r_k'], prm['seg_red'], prm['seg_exp'], s0)


def _bias_kernel(rb_ref, o_ref, tv_ref, *, tq, nk, width):
    h = pl.program_id(0)
    nrel = rb_ref.shape[1]
    m = lax.broadcasted_iota(jnp.int32, (nrel, width), 1)
    r = lax.broadcasted_iota(jnp.int32, (nrel, width), 0)
    m = jnp.where(m >= nk, m - width, m)
    sel = (jnp.clip(PREV_ROWS - m, -REL_CLIP, REL_CLIP) + REL_CLIP == r).astype(F32)
    tv_ref[...] = jnp.dot(rb_ref[...], sel, precision=lax.Precision.HIGHEST, preferred_element_type=F32)
    row = jnp.broadcast_to(tv_ref[pl.ds(h, 1), :], (tq, width))
    t = pltpu.roll(row, 0, axis=1, stride=1, stride_axis=0)[:, :nk]
    shift = CHUNK.bit_length() - 1
    col = lax.broadcasted_iota(jnp.int32, (tq, nk), 1)
    ci = lax.shift_right_logical(lax.broadcasted_iota(jnp.int32, (tq, nk), 0), shift)
    cj = lax.shift_right_logical(col, shift)
    band = jnp.where((cj >= ci) & (cj <= ci + N_PREV_CHUNKS), t, NEG_INF)
    for var in range(o_ref.shape[0]):
        o_ref[var] = jnp.where(col + var * tq - PREV_ROWS >= 0, band, NEG_INF)


def _bias_tile(rel_bias, *, tq):
    nh, nrel = rel_bias.shape
    nk = PREV_ROWS + tq
    n_var = PREV_ROWS // tq + 1
    width = -(-(nk + tq) // LANES) * LANES
    nrel_pad = -(-nrel // LANES) * LANES
    rb = jnp.pad(rel_bias, ((0, 0), (0, nrel_pad - nrel)))
    return pl.pallas_call(
        functools.partial(_bias_kernel, tq=tq, nk=nk, width=width),
        grid=(nh,),
        in_specs=[pl.BlockSpec((nh, nrel_pad), lambda h: (0, 0))],
        out_specs=pl.BlockSpec((None, n_var, tq, nk), lambda h: (h, 0, 0, 0)),
        out_shape=jax.ShapeDtypeStruct((nh, n_var, tq, nk), F32),
        scratch_shapes=[pltpu.VMEM((nh, width), F32)],
        compiler_params=_cparams(("arbitrary",), 32),
        name="band_bias",
    )(rb)


def _softmax_parts(s):
    p = jnp.exp(s - jnp.max(s, axis=-1, keepdims=True))
    return p.astype(BF16), jnp.sum(p, axis=-1, keepdims=True)


def _attn_prompt_kernel(*refs, nkb, dh):
    q_ref = refs[0]
    k_refs, v_refs = refs[1:1 + nkb], refs[1 + nkb:1 + 2 * nkb]
    bias_ref, o_ref = refs[1 + 2 * nkb:]
    for hh in range(ATT_HEADS_PER_STEP):
        cols = slice(hh * dh, (hh + 1) * dh)
        k = jnp.concatenate([kr[:, cols] for kr in k_refs], axis=0)
        v = jnp.concatenate([vr[:, cols] for vr in v_refs], axis=0)
        p, l = _softmax_parts(_dot_nt(q_ref[:, cols], k) + bias_ref[hh])
        o_ref[:, cols] = (jnp.dot(p, v, preferred_element_type=F32) / l).astype(o_ref.dtype)


def _attn_prompt(qkv, bias, *, nh, dh):
    t = qkv.shape[0]
    tq, hs = ATT_TQ, ATT_HEADS_PER_STEP
    assert tq % CHUNK == 0 and PREV_ROWS % tq == 0 and nh % hs == 0
    nkb = PREV_ROWS // tq + 1
    nhp = nh // hs

    def kv_spec(which, back):
        return pl.BlockSpec((tq, hs * dh), lambda h, qb: (jnp.maximum(qb - back, 0), which * nhp + h))

    return pl.pallas_call(
        functools.partial(_attn_prompt_kernel, nkb=nkb, dh=dh),
        grid=(nhp, t // tq),
        in_specs=[pl.BlockSpec((tq, hs * dh), lambda h, qb: (qb, h))]
                 + [kv_spec(1, nkb - 1 - j) for j in range(nkb)]
                 + [kv_spec(2, nkb - 1 - j) for j in range(nkb)]
                 + [pl.BlockSpec((hs, None, tq, PREV_ROWS + tq), lambda h, qb: (h, jnp.minimum(qb, nkb - 1), 0, 0))],
        out_specs=pl.BlockSpec((tq, hs * dh), lambda h, qb: (qb, h)),
        out_shape=jax.ShapeDtypeStruct((t, nh * dh), BF16),
        compiler_params=_cparams(("parallel", "parallel"), 32),
        name="band_attn_prompt",
    )(qkv, *([qkv] * (2 * nkb)), bias)


def _attn_sample_kernel(q_ref, kc_ref, kn_ref, vc_ref, vn_ref, bias_ref, o_ref, *, dh):
    for hh in range(ATT_HEADS_PER_STEP):
        cols = slice(hh * dh, (hh + 1) * dh)
        k = jnp.concatenate([kc_ref[:, :, cols], kn_ref[:, :, cols]], axis=1).astype(BF16)
        v = jnp.concatenate([vc_ref[:, :, cols], vn_ref[:, :, cols]], axis=1).astype(BF16)
        s = jnp.einsum('bqd,bkd->bqk', q_ref[:, :, cols].astype(BF16), k, preferred_element_type=F32)
        p, l = _softmax_parts(s + bias_ref[hh][None])
        o = jnp.einsum('bqk,bkd->bqd', p, v, preferred_element_type=F32) / l
        o_ref[:, :, cols] = o.astype(o_ref.dtype)


def _attn_sample(q, kc, kn, vc, vn, bias, *, nh, dh):
    b, t, d = q.shape
    nr = kc.shape[1]
    hs = ATT_HEADS_PER_STEP
    new = pl.BlockSpec((b, t, hs * dh), lambda h: (0, 0, h))
    old = pl.BlockSpec((b, nr, hs * dh), lambda h: (0, 0, h))
    return pl.pallas_call(
        functools.partial(_attn_sample_kernel, dh=dh),
        grid=(nh // hs,),
        in_specs=[new, old, new, old, new, pl.BlockSpec((hs, t, nr + t), lambda h: (h, 0, 0))],
        out_specs=new,
        out_shape=jax.ShapeDtypeStruct((b, t, d), BF16),
        compiler_params=_cparams(("parallel",), 48),
        name="band_attn_sample",
    )(q, kc, kn, vc, vn, bias)


def _run_group(x3, mods, pool_hist, shift_prev, wkv0, cache, pos0, p):
    bsz, t, d = x3.shape
    rows = bsz * t
    dp = pool_hist.shape[-1]
    ds = shift_prev.shape[-1]
    nh_att = p['rel_bias'].shape[0]
    dh = d // nh_att
    x = x3.reshape(rows, d)

    sh1, sc1, gt1, sh2, sc2, gt2 = mods[0]
    gn = p['g_norm'][0]
    proj = _norm_mm(x, gn[0:1], sc1, sh1, p['w_in'], tm=512, tn_cap=2304)
    assert t >= POOL_PAD - 1
    hist_pad = jnp.pad(pool_hist, ((0, 0), (POOL_PAD - pool_hist.shape[1], 0), (0, 0)))
    pool_out = _pool_mix(proj, ds, hist_pad, p['w_pool'], p['pool_scale'], bsz=bsz, t=t, pos0=pos0)
    zfirst = jnp.pad(shift_prev[:, None, :], ((0, 0), (SUBLANES - 1, 0), (0, 0)))
    yg, wkv_new = _rwkv_mix(proj, zfirst, wkv0, p, t=t)
    x = _mm_norm_res([pool_out, yg], p['w_out0'], gn[1:2], gt1, x)
    a = _norm_mm(x, gn[2:3], sc2, sh2, p['w_ff1'], layer=0, relu2=True, out_dtype=BF16, tn_cap=2048)
    x = _mm_norm_res([a], p['w_ff2'], gn[3:4], gt2, x, layer=0)
    tail = proj.reshape(bsz, t, -1)[:, t - (POOL_PAD - 1):]
    pool_new = tail[:, :, ds:]
    shift_new = tail[:, POOL_PAD - 2, :ds]

    sh1, sc1, gt1, sh2, sc2, gt2 = mods[1]
    gn = p['g_norm'][1]
    if cache is None:
        assert bsz == 1 and pos0 == 0
        qkv = _norm_mm(x, gn[0:1], sc1, sh1, p['w_qkv1'], out_dtype=BF16)
        o = _attn_prompt(qkv, p['bias_tile'], nh=nh_att, dh=dh)
        keep = min(PREV_ROWS, t)
        kv = _norm_mm(x[t - keep:], gn[0:1], sc1, sh1, p['w_qkv1'], col0=d)
        k_new = kv[:, :d].reshape(1, keep, nh_att, dh)
        v_new = kv[:, d:].reshape(1, keep, nh_att, dh)
    else:
        cache_k, cache_v = cache
        nr = cache_k.shape[1]
        assert nr == PREV_ROWS and pos0 % CHUNK == 0 and pos0 >= nr and t <= CHUNK
        qkv3 = _norm_mm(x, gn[0:1], sc1, sh1, p['w_qkv1']).reshape(bsz, t, 3 * d)
        k_new, v_new = qkv3[:, :, d:2 * d], qkv3[:, :, 2 * d:]
        bias = p['bias_tile'][:, -1, :t, :nr + t]
        o = _attn_sample(qkv3[:, :, :d], cache_k.reshape(bsz, nr, d), k_new, cache_v.reshape(bsz, nr, d), v_new,
                         bias, nh=nh_att, dh=dh).reshape(rows, d)
        k_new = k_new.reshape(bsz, t, nh_att, dh)
        v_new = v_new.reshape(bsz, t, nh_att, dh)
    x = _mm_norm_res([o], p['w_out1'], gn[1:2], gt1, x)
    a = _norm_mm(x, gn[2:3], sc2, sh2, p['w_ff1'], layer=1, relu2=True, out_dtype=BF16, tn_cap=2048)
    x = _mm_norm_res([a], p['w_ff2'], gn[3:4], gt2, x, layer=1)
    return x.reshape(bsz, t, d), pool_new, shift_new, wkv_new, k_new, v_new


def kernel(x_prompt, x_sample, c_prompt, c_sample, state_l0_pool, state_l0_shift, state_l0_wkv, cache_l1_k, cache_l1_v, w_ada, b_ada, g_norm, w_in0, w_pool, pool_scale, mu_shift, w0_decay, w2_decay, a0_iclr, a2_iclr, g2_gate, k_k, k_a, r_k, lnx_w, lnx_b, w_out0, w_qkv1, rel_bias, w_out1, w_ff1, w_ff2):
    bp, tp, d = x_prompt.shape
    bs, ts, _ = x_sample.shape
    depth = w_ada.shape[0]
    dp = state_l0_pool.shape[-1]
    dr = d - dp
    nh = dr // RWKV_HEAD
    nh_att = rel_bias.shape[0]
    n_dec, n_iclr, n_gate = w2_decay.shape[0], a2_iclr.shape[0], g2_gate.shape[0]
    row = lambda a: a.reshape(1, -1)

    n_c = bp + bs
    c_all = jnp.pad(jnp.concatenate([c_prompt, c_sample], axis=0), ((0, -n_c % SUBLANES), (0, 0)))
    mod = _ada_mod(c_all, w_ada, b_ada)
    mods_p = [[mod[l, 0:bp, i * d:(i + 1) * d] for i in range(6)] for l in range(depth)]
    mods_s = [[jnp.repeat(mod[l, bp:n_c, i * d:(i + 1) * d], ts, axis=0) for i in range(6)] for l in range(depth)]

    seg = (lax.broadcasted_iota(jnp.int32, (dr, LANES), 0) // RWKV_HEAD
           == lax.broadcasted_iota(jnp.int32, (dr, LANES), 1)).astype(BF16)
    w_lora = jnp.zeros((3, n_dec + n_iclr + n_gate, dr), F32)
    w_lora = w_lora.at[0, :n_dec].set(w2_decay).at[1, n_dec:n_dec + n_iclr].set(a2_iclr)
    w_lora = w_lora.at[2, n_dec + n_iclr:].set(g2_gate)
    q_scale = jnp.concatenate([jnp.full((d,), (d // nh_att) ** -0.5, F32), jnp.ones((2 * d,), F32)])
    p = {
        'g_norm': g_norm,
        'w_in': jnp.concatenate([w_in0[:, dp:], w_in0[:, :dp]], axis=1).astype(BF16),
        'w_pool': w_pool.astype(BF16), 'pool_scale': row(pool_scale), 'mu_shift': row(mu_shift),
        'w_lora': w_lora.astype(BF16), 'w0_decay': row(w0_decay), 'a0_iclr': row(a0_iclr),
        'k_k': row(k_k), 'k_a': row(k_a), 'r_k': row(r_k), 'lnx_w': row(lnx_w), 'lnx_b': row(lnx_b),
        'seg_red': seg, 'seg_exp': seg.T,
        'w_out0': w_out0.astype(BF16), 'w_qkv1': (w_qkv1 * q_scale).astype(BF16), 'w_out1': w_out1.astype(BF16),
        'w_ff1': w_ff1.astype(BF16), 'w_ff2': w_ff2.astype(BF16),
        'rel_bias': rel_bias, 'bias_tile': _bias_tile(rel_bias, tq=ATT_TQ),
    }

    y_p, pool_p, shift_p, wkv_p, k_p, v_p = _run_group(
        x_prompt, mods_p, jnp.zeros((bp, POOL_PAD - 1, dp), F32), jnp.zeros((bp, w_in0.shape[1] - dp), F32),
        jnp.zeros((bp, nh, RWKV_HEAD, RWKV_HEAD), F32), None, 0, p)
    y_s, pool_s, shift_s, wkv_s, k_s, v_s = _run_group(
        x_sample, mods_s, state_l0_pool, state_l0_shift, state_l0_wkv, (cache_l1_k, cache_l1_v), PAST_LEN, p)
    return (y_p, y_s, pool_p, pool_s, shift_p, shift_s, wkv_p, wkv_s, k_p, v_p, k_s, v_s)
```

```python
import functools

import jax
import jax.numpy as jnp
from jax import lax
from jax.experimental import pallas as pl
from jax.experimental.pallas import tpu as pltpu

F32 = jnp.float32
BF16 = jnp.bfloat16

NORM_EPS = 1e-6
LNX_EPS = 64e-5
CHUNK = 64
POOL_WINDOWS = (2, 4, 8, 16)
POOL_PAD = 16
RWKV_HEAD = 64
DECAY_SCALE = 0.6065306597126334
WKV_CHUNK = 128
N_PREV_CHUNKS = 8
PREV_ROWS = N_PREV_CHUNKS * CHUNK
REL_CLIP = 2 * CHUNK
PAST_LEN = 2048
NEG_INF = -1e30
V7X_VMEM_BYTES = 64 * 1024 * 1024
LANES = 128
SUBLANES = 8
ATT_TQ = 256
ATT_HEADS_PER_STEP = 4


def _cparams(semantics, vmem_mb):
    assert vmem_mb * 1024 * 1024 <= V7X_VMEM_BYTES
    return pltpu.CompilerParams(dimension_semantics=semantics, vmem_limit_bytes=vmem_mb * 1024 * 1024)


def _col_tile(n, cap):
    best = None
    for t in range(LANES, min(n, cap) + 1, LANES):
        if n % t == 0:
            best = t
    assert best is not None
    return best


def _dot(a, b):
    return jnp.dot(a.astype(BF16), b.astype(BF16), preferred_element_type=F32)


def _dot_nt(a, b):
    return lax.dot_general(a.astype(BF16), b.astype(BF16), (((1,), (1,)), ((), ())), preferred_element_type=F32)


def _split2(x):
    hi = x.astype(BF16)
    return hi, (x - hi.astype(F32)).astype(BF16)


def _split3(x):
    hi = x.astype(BF16)
    r1 = x - hi.astype(F32)
    mid = r1.astype(BF16)
    lo = (r1 - mid.astype(F32)).astype(BF16)
    return hi, mid, lo


def _segsum(x, red, exp):
    hi, lo = _split2(jnp.dot(x.astype(BF16), red, preferred_element_type=F32))
    return jnp.dot(hi, exp, preferred_element_type=F32) + jnp.dot(lo, exp, preferred_element_type=F32)


def _sigmoid(x):
    return 1.0 / (1.0 + jnp.exp(-x))


def _ada_kernel(c_ref, w_ref, b_ref, o_ref):
    c = c_ref[...]
    o_ref[...] = _dot(c * _sigmoid(c), w_ref[...]) + b_ref[...]


def _ada_mod(c_all, w_ada, b_ada, *, tn=1024):
    depth, d, n = w_ada.shape
    rows = c_all.shape[0]
    return pl.pallas_call(
        _ada_kernel,
        grid=(depth, n // tn),
        in_specs=[pl.BlockSpec((rows, d), lambda l, j: (0, 0)),
                  pl.BlockSpec((None, d, tn), lambda l, j: (l, 0, j)),
                  pl.BlockSpec((None, 1, tn), lambda l, j: (l, 0, j))],
        out_specs=pl.BlockSpec((None, rows, tn), lambda l, j: (l, 0, j)),
        out_shape=jax.ShapeDtypeStruct((depth, rows, n), F32),
        compiler_params=_cparams(("parallel", "parallel"), 40),
        name="ada_mod",
    )(c_all, w_ada, b_ada.reshape(depth, 1, n))


def _modulated_norm(x, g_ref, sc_ref, sh_ref):
    y = x * lax.rsqrt(jnp.mean(x * x, axis=-1, keepdims=True) + NORM_EPS) * g_ref[...]
    return (y * (1.0 + sc_ref[...]) + sh_ref[...]).astype(BF16)


def _mm_store(h_ref, w_ref, o_ref, relu2):
    acc = jnp.dot(h_ref[...], w_ref[...], preferred_element_type=F32)
    if relu2:
        acc = jnp.square(jnp.maximum(acc, 0.0))
    o_ref[...] = acc.astype(o_ref.dtype)


def _norm_mm_kernel(x_ref, g_ref, sc_ref, sh_ref, w_ref, o_ref, h_ref, *, relu2):
    @pl.when(pl.program_id(1) == 0)
    def _():
        h_ref[...] = _modulated_norm(x_ref[...], g_ref, sc_ref, sh_ref)

    _mm_store(h_ref, w_ref, o_ref, relu2)


def _norm_mm_ahead_kernel(x0_ref, xn_ref, g_ref, sc_ref, sh_ref, w_ref, o_ref, ha_ref, hb_ref, *, relu2, rps):
    i, j = pl.program_id(0), pl.program_id(1)

    @pl.when((i == 0) & (j == 0))
    def _():
        ha_ref[...] = _modulated_norm(x0_ref[...], g_ref, sc_ref, sh_ref)

    def step(cur_ref, nxt_ref):
        _mm_store(cur_ref, w_ref, o_ref, relu2)
        rows = pl.ds(pl.multiple_of(j * rps, rps), rps)
        nxt_ref[rows, :] = _modulated_norm(xn_ref[rows, :], g_ref, sc_ref, sh_ref)

    @pl.when(lax.rem(i, 2) == 0)
    def _():
        step(ha_ref, hb_ref)

    @pl.when(lax.rem(i, 2) == 1)
    def _():
        step(hb_ref, ha_ref)


def _norm_mm_ahead(x, g, sc, sh, w, *, layer, col0, relu2, out_dtype, tm=512, tn_cap=2048):
    rows, d = x.shape
    n = w.shape[-1] - col0
    ni = rows // tm
    tn = max(t for t in range(LANES, min(n, tn_cap) + 1, LANES)
             if n % t == 0 and tm % (n // t) == 0 and (tm // (n // t)) % (2 * SUBLANES) == 0)
    nj = n // tn
    assert col0 % tn == 0 and sc.shape[0] == 1
    cb = col0 // tn
    vec = pl.BlockSpec((1, d), lambda i, j: (0, 0))
    w_spec = (pl.BlockSpec((d, tn), lambda i, j: (0, cb + j)) if layer is None
              else pl.BlockSpec((None, d, tn), lambda i, j: (layer, 0, cb + j)))
    return pl.pallas_call(
        functools.partial(_norm_mm_ahead_kernel, relu2=relu2, rps=tm // nj),
        grid=(ni, nj),
        in_specs=[pl.BlockSpec((tm, d), lambda i, j: (0, 0), pipeline_mode=pl.Buffered(1)),
                  pl.BlockSpec((tm, d), lambda i, j: (jnp.minimum(i + 1, ni - 1), 0)),
                  vec, vec, vec, w_spec],
        out_specs=pl.BlockSpec((tm, tn), lambda i, j: (i, j)),
        out_shape=jax.ShapeDtypeStruct((rows, n), out_dtype),
        scratch_shapes=[pltpu.VMEM((tm, d), BF16), pltpu.VMEM((tm, d), BF16)],
        compiler_params=_cparams(("arbitrary", "arbitrary"), 56),
        name="norm_mm_ahead",
    )(x, x, g, sc, sh, w)


def _norm_mm(x, g, sc, sh, w, *, layer=None, col0=0, relu2=False, out_dtype=F32, tm=1024, tn_cap=1024,
             ahead=False):
    rows, d = x.shape
    if ahead and rows >= 2 * tm and sc.shape[0] == 1:
        return _norm_mm_ahead(x, g, sc, sh, w, layer=layer, col0=col0, relu2=relu2, out_dtype=out_dtype,
                              tn_cap=max(tn_cap, 2048))
    n = w.shape[-1] - col0
    tm = min(rows, tm)
    tn = _col_tile(n, tn_cap if rows > tm else 2 * tn_cap)
    assert col0 % tn == 0
    cb = col0 // tn
    mod_spec = (pl.BlockSpec((tm, d), lambda i, j: (i, 0)) if sc.shape[0] == rows
                else pl.BlockSpec((1, d), lambda i, j: (0, 0)))
    w_spec = (pl.BlockSpec((d, tn), lambda i, j: (0, cb + j)) if layer is None
              else pl.BlockSpec((None, d, tn), lambda i, j: (layer, 0, cb + j)))
    return pl.pallas_call(
        functools.partial(_norm_mm_kernel, relu2=relu2),
        grid=(rows // tm, n // tn),
        in_specs=[pl.BlockSpec((tm, d), lambda i, j: (i, 0)),
                  pl.BlockSpec((1, d), lambda i, j: (0, 0)),
                  mod_spec, mod_spec, w_spec],
        out_specs=pl.BlockSpec((tm, tn), lambda i, j: (i, j)),
        out_shape=jax.ShapeDtypeStruct((rows, n), out_dtype),
        scratch_shapes=[pltpu.VMEM((tm, d), BF16)],
        compiler_params=_cparams(("parallel", "arbitrary"), 56),
        name="norm_mm",
    )(x, g, sc, sh, w)


def _norm_res_epilogue(o, g_ref, gate_ref, x_ref, o_ref):
    y = o * lax.rsqrt(jnp.mean(o * o, axis=-1, keepdims=True) + NORM_EPS) * g_ref[...]
    o_ref[...] = x_ref[...] + gate_ref[...] * y


def _mm_norm_res_kernel(a_ref, w_ref, g_ref, gate_ref, x_ref, o_ref, acc_ref):
    k = pl.program_id(1)

    @pl.when(k == 0)
    def _():
        acc_ref[...] = jnp.zeros_like(acc_ref)

    acc_ref[...] += jnp.dot(a_ref[...], w_ref[...], preferred_element_type=F32)

    @pl.when(k == pl.num_programs(1) - 1)
    def _():
        _norm_res_epilogue(acc_ref[...], g_ref, gate_ref, x_ref, o_ref)


def _mm_norm_res_1step_kernel(*refs, n_a):
    a_refs = refs[:n_a]
    w_ref, g_ref, gate_ref, x_ref, o_ref = refs[n_a:]
    ka = a_refs[0].shape[1]
    o = jnp.dot(a_refs[0][...], w_ref[0:ka, :], preferred_element_type=F32)
    for idx in range(1, n_a):
        o = o + jnp.dot(a_refs[idx][...], w_ref[idx * ka:(idx + 1) * ka, :], preferred_element_type=F32)
    _norm_res_epilogue(o, g_ref, gate_ref, x_ref, o_ref)


def _mm_norm_res_lag_kernel(*refs, n_a, rps):
    a_refs = refs[:n_a]
    w_ref, g_ref, gate_ref, x_ref, o_ref, acc_a, acc_b = refs[n_a:]
    i, k = pl.program_id(0), pl.program_id(1)
    ka = a_refs[0].shape[1]

    @pl.when((i == 0) & (k == 0))
    def _():
        acc_a[...] = jnp.zeros_like(acc_a)
        acc_b[...] = jnp.zeros_like(acc_b)

    def step(cur_ref, prev_ref):
        part = jnp.dot(a_refs[0][...], w_ref[0:ka, :], preferred_element_type=F32)
        for idx in range(1, n_a):
            part = part + jnp.dot(a_refs[idx][...], w_ref[idx * ka:(idx + 1) * ka, :], preferred_element_type=F32)
        cur_ref[...] = jnp.where(k == 0, part, cur_ref[...] + part)
        rows = pl.ds(pl.multiple_of(k * rps, rps), rps)
        o = prev_ref[rows, :]
        y = o * lax.rsqrt(jnp.mean(o * o, axis=-1, keepdims=True) + NORM_EPS) * g_ref[...]
        o_ref[rows, :] = x_ref[rows, :] + gate_ref[...] * y

    @pl.when(lax.rem(i, 2) == 0)
    def _():
        step(acc_a, acc_b)

    @pl.when(lax.rem(i, 2) == 1)
    def _():
        step(acc_b, acc_a)


def _mm_norm_res_lag(a_list, w, g, gate, x, *, layer, tm, tk, nk):
    rows, d = x.shape
    ni = rows // tm
    n_a = len(a_list)
    ka = a_list[0].shape[1]
    assert gate.shape[0] == 1 and tm % nk == 0 and (tm // nk) % SUBLANES == 0
    lag_spec = pl.BlockSpec((tm, d), lambda i, k: (jnp.maximum(i - 1, 0), 0))
    vec = pl.BlockSpec((1, d), lambda i, k: (0, 0))
    w_spec = (pl.BlockSpec((tk, d), lambda i, k: (k, 0)) if layer is None
              else pl.BlockSpec((None, tk, d), lambda i, k: (layer, k, 0)))
    a_specs = [pl.BlockSpec((tm, ka if nk == 1 else tk), lambda i, k: (jnp.minimum(i, ni - 1), k)) for _ in a_list]
    return pl.pallas_call(
        functools.partial(_mm_norm_res_lag_kernel, n_a=n_a, rps=tm // nk),
        grid=(ni + 1, nk),
        in_specs=a_specs + [w_spec, vec, vec, lag_spec],
        out_specs=lag_spec,
        out_shape=jax.ShapeDtypeStruct((rows, d), F32),
        scratch_shapes=[pltpu.VMEM((tm, d), F32), pltpu.VMEM((tm, d), F32)],
        compiler_params=_cparams(("arbitrary", "arbitrary"), 56),
        name="mm_norm_res_lag",
    )(*a_list, w, g, gate, x)


def _mm_norm_res(a_list, w, g, gate, x, *, layer=None, tk_cap=2048):
    rows, d = x.shape
    tm = min(rows, 512)
    if rows >= 2 * tm and gate.shape[0] == 1 and len(a_list) == 1 and a_list[0].shape[1] > tk_cap:
        return _mm_norm_res_lag(a_list, w, g, gate, x, layer=layer, tm=tm, tk=tk_cap,
                                nk=a_list[0].shape[1] // tk_cap)
    ka = a_list[0].shape[1]
    n_a = len(a_list)
    k_all = n_a * ka
    tk = min(k_all, tk_cap)
    nk = k_all // tk
    assert nk == 1 or n_a == 1
    row_spec = pl.BlockSpec((tm, d), lambda i, k: (i, 0))
    gate_spec = row_spec if gate.shape[0] == rows else pl.BlockSpec((1, d), lambda i, k: (0, 0))
    w_spec = (pl.BlockSpec((tk, d), lambda i, k: (k, 0)) if layer is None
              else pl.BlockSpec((None, tk, d), lambda i, k: (layer, k, 0)))
    a_specs = [pl.BlockSpec((tm, ka if nk == 1 else tk), lambda i, k: (i, k)) for _ in a_list]
    return pl.pallas_call(
        functools.partial(_mm_norm_res_1step_kernel, n_a=n_a) if nk == 1 else _mm_norm_res_kernel,
        grid=(rows // tm, nk),
        in_specs=a_specs + [w_spec, pl.BlockSpec((1, d), lambda i, k: (0, 0)), gate_spec, row_spec],
        out_specs=row_spec,
        out_shape=jax.ShapeDtypeStruct((rows, d), F32),
        scratch_shapes=[] if nk == 1 else [pltpu.VMEM((tm, d), F32)],
        compiler_params=_cparams(("parallel", "arbitrary"), 56),
        name="mm_norm_res",
    )(*a_list, w, g, gate, x)


def _pool_kernel(*refs, tm, pos0):
    ng = len(POOL_WINDOWS)
    u_refs, prev_refs = refs[:ng], refs[ng:2 * ng]
    hist_ref, w_ref, ps_ref, o_ref = refs[2 * ng:]
    i = pl.program_id(1)
    gc = u_refs[0].shape[1]
    pos = (pos0 + i * tm + lax.broadcasted_iota(jnp.int32, (tm, gc), 0)).astype(F32)
    for gi, win in enumerate(POOL_WINDOWS):
        prev = jnp.where(i == 0, hist_ref[:, gi * gc:(gi + 1) * gc], prev_refs[gi][...])
        u = u_refs[gi][...]
        s, d = jnp.concatenate([prev, u], axis=0), 1
        while d < win:
            s = s + pltpu.roll(s, d, axis=0)
            d *= 2
        cnt = jnp.minimum(F32(win), pos + 1.0)
        dev = s[POOL_PAD:, :] / cnt - u
        o = _dot(dev, w_ref[gi]) * ps_ref[:, gi * gc:(gi + 1) * gc]
        o_ref[:, gi * gc:(gi + 1) * gc] = o.astype(o_ref.dtype)


def _pool_mix(p_all, col0, hist_pad, w_pool, pool_scale, *, bsz, t, pos0):
    dp = hist_pad.shape[-1]
    ng = len(POOL_WINDOWS)
    gc = dp // ng
    tm = min(t, 512)
    assert max(POOL_WINDOWS) <= POOL_PAD and tm % POOL_PAD == 0 and col0 % gc == 0
    nt, blk, cb = t // tm, tm // POOL_PAD, col0 // gc
    cur = [pl.BlockSpec((tm, gc), lambda b, i, gi=gi: (b * nt + i, cb + gi)) for gi in range(ng)]
    prev = [pl.BlockSpec((POOL_PAD, gc), lambda b, i, gi=gi: (b * nt * blk + jnp.maximum(i * blk - 1, 0), cb + gi))
            for gi in range(ng)]
    return pl.pallas_call(
        functools.partial(_pool_kernel, tm=tm, pos0=pos0),
        grid=(bsz, nt),
        in_specs=cur + prev + [pl.BlockSpec((None, POOL_PAD, dp), lambda b, i: (b, 0, 0)),
                               pl.BlockSpec(w_pool.shape, lambda b, i: (0, 0, 0)),
                               pl.BlockSpec((1, dp), lambda b, i: (0, 0))],
        out_specs=pl.BlockSpec((tm, dp), lambda b, i: (b * nt + i, 0)),
        out_shape=jax.ShapeDtypeStruct((bsz * t, dp), BF16),
        compiler_params=_cparams(("parallel", "parallel"), 32),
        name="pool_mix",
    )(*([p_all] * (2 * ng)), hist_pad, w_pool, pool_scale)


def _rwkv_token_prep(z, prev_row, mu_ref, wl_ref, w0_ref, a0_ref, kk_ref, ka_ref, red, exp, *, dr):
    rows = lax.broadcasted_iota(jnp.int32, z.shape, 0)
    z_prev = jnp.where(rows == 0, prev_row, pltpu.roll(z, 1, axis=0))
    zs = z + (z_prev - z) * mu_ref[...]
    r, k, v, xl = zs[:, :dr], zs[:, dr:2 * dr], zs[:, 2 * dr:3 * dr], zs[:, 3 * dr:]
    log_decay = -DECAY_SCALE * _sigmoid(w0_ref[...] + _dot(jnp.tanh(xl), wl_ref[0]))
    a = _sigmoid(a0_ref[...] + _dot(xl, wl_ref[1]))
    g = _dot(_sigmoid(xl), wl_ref[2])
    kk = k * kk_ref[...]
    kk = kk / jnp.maximum(jnp.sqrt(_segsum(kk * kk, red, exp)), 1e-12)
    return r, log_decay, k * (1.0 + (a - 1.0) * ka_ref[...]), v, kk, kk * a, g


def _wkv_chunk(r, lw, k, v, kk, b, s, *, nh, hd):
    n_tok = lw.shape[0]
    ti = lax.broadcasted_iota(jnp.int32, (n_tok, n_tok), 0)
    tj = lax.broadcasted_iota(jnp.int32, (n_tok, n_tok), 1)
    tri = (ti >= tj).astype(BF16)
    hi, mid, lo = _split3(lw)
    cum = (jnp.dot(tri, hi, preferred_element_type=F32) + jnp.dot(tri, mid, preferred_element_type=F32)
           + jnp.dot(tri, lo, preferred_element_type=F32))
    tot = cum[n_tok - 1:n_tok, :]
    e_neg = jnp.exp(-cum)
    e_rem = jnp.exp(tot - cum)

    def heads(x):
        return jnp.stack([x[:, h * hd:(h + 1) * hd] for h in range(nh)])

    aw = heads(-kk * jnp.exp(cum - lw))
    rw = heads(r * jnp.exp(cum))
    bw = heads(b * e_neg)
    kw = heads(k * e_neg)
    bh = heads(b * e_rem)
    kh = heads(k * e_rem)
    v = heads(v)
    w_tot = heads(jnp.exp(tot))

    def bmm(spec, x, y):
        return jnp.einsum(spec, x.astype(BF16), y.astype(BF16), preferred_element_type=F32)

    strict = (ti > tj)[None]
    incl = (ti >= tj)[None]
    aw_rw = jnp.concatenate([aw, rw], axis=1)
    on_b = bmm('hlk,hmk->hlm', aw_rw, bw)
    on_k = bmm('hlk,hmk->hlm', aw_rw, kw)
    a_ab = jnp.where(strict, on_b[:, :n_tok], 0.0)
    a_ak = jnp.where(strict, on_k[:, :n_tok], 0.0)
    a_rb_rk = jnp.concatenate([jnp.where(incl, on_b[:, n_tok:], 0.0), jnp.where(incl, on_k[:, n_tok:], 0.0)], axis=2)
    eye = jnp.where(ti == tj, 1.0, 0.0)[None]
    t_inv, blk = eye, 1
    while blk < n_tok:
        sh = blk.bit_length() - 1
        pair = ((jnp.right_shift(ti, sh + 1) == jnp.right_shift(tj, sh + 1))
                & ((jnp.right_shift(ti, sh) & 1) == 1) & ((jnp.right_shift(tj, sh) & 1) == 0))
        m = jnp.where(pair[None], a_ab, 0.0)
        if blk == 1:
            t_inv = t_inv + m
        else:
            t_inv = t_inv + bmm('hlm,hmv->hlv', t_inv, bmm('hlm,hmv->hlv', m, t_inv))
        blk *= 2
    rhs = jnp.concatenate([aw, bmm('hlm,hmv->hlv', a_ak, v)], axis=2)
    x = rhs + bmm('hlm,hmv->hlv', t_inv - eye, rhs)
    a_bar, u_c = x[:, :, :hd], x[:, :, hd:]

    on_s = bmm('hlk,hvk->hlv', jnp.concatenate([a_bar, rw], axis=1), s)
    uv = jnp.concatenate([on_s[:, :n_tok] + u_c, v], axis=1)
    y = on_s[:, n_tok:] + bmm('hlm,hmv->hlv', a_rb_rk, uv)
    s_new = s * w_tot + bmm('hlv,hlk->hvk', uv, jnp.concatenate([bh, kh], axis=1))
    return jnp.concatenate([y[h] for h in range(nh)], axis=1), s_new


def _rwkv_kernel(z_ref, zfirst_ref, mu_ref, wl_ref, w0_ref, a0_ref, kk_ref, ka_ref, lnw_ref, lnb_ref, rk_ref,
                 red_ref, exp_ref, s0_ref, o_ref, sout_ref, s_ref, zlast_ref, *, nh, hd):
    c = pl.program_id(1)

    @pl.when(c == 0)
    def _():
        s_ref[...] = s0_ref[...]
        zlast_ref[...] = zfirst_ref[...]

    red, exp = red_ref[...], exp_ref[...]
    z = z_ref[...]
    n_tok = z.shape[0]
    r, lw, k, v, kk, b, g = _rwkv_token_prep(z, zlast_ref[SUBLANES - 1:, :], mu_ref, wl_ref, w0_ref, a0_ref,
                                             kk_ref, ka_ref, red, exp, dr=nh * hd)
    zlast_ref[...] = z[n_tok - SUBLANES:, :]
    y, s_new = _wkv_chunk(r, lw, k, v, kk, b, s_ref[...], nh=nh, hd=hd)
    s_ref[...] = s_new

    mu = _segsum(y, red, exp) * (1.0 / hd)
    yc = y - mu
    var = _segsum(yc * yc, red, exp) * (1.0 / hd)
    yn = yc * lax.rsqrt(var + LNX_EPS) * lnw_ref[...] + lnb_ref[...]
    bonus = _segsum(r * k * rk_ref[...], red, exp) * v
    o_ref[...] = ((yn + bonus) * g).astype(o_ref.dtype)

    @pl.when(c == pl.num_programs(1) - 1)
    def _():
        sout_ref[...] = s_new


def _rwkv_mix(p_all, zfirst, s0, prm, *, t):
    ds = zfirst.shape[-1]
    bsz, nh, hd = s0.shape[0], s0.shape[1], s0.shape[2]
    dr = nh * hd
    n_tok = min(t, WKV_CHUNK)
    nc = t // n_tok
    assert n_tok % SUBLANES == 0
    vec = pl.BlockSpec((1, dr), lambda bi, c: (0, 0))
    tile = lambda w: pl.BlockSpec((n_tok, w), lambda bi, c: (bi * nc + c, 0))
    st = pl.BlockSpec((None, nh, hd, hd), lambda bi, c: (bi, 0, 0, 0))
    whole = lambda a: pl.BlockSpec(a.shape, lambda bi, c: (0,) * a.ndim)
    return pl.pallas_call(
        functools.partial(_rwkv_kernel, nh=nh, hd=hd),
        grid=(bsz, nc),
        in_specs=[tile(ds), pl.BlockSpec((None, SUBLANES, ds), lambda bi, c: (bi, 0, 0)),
                  pl.BlockSpec((1, ds), lambda bi, c: (0, 0)), whole(prm['w_lora'])] + [vec] * 7
                 + [whole(prm['seg_red']), whole(prm['seg_exp']), st],
        out_specs=[tile(dr), st],
        out_shape=[jax.ShapeDtypeStruct((bsz * t, dr), BF16), jax.ShapeDtypeStruct(s0.shape, F32)],
        scratch_shapes=[pltpu.VMEM((nh, hd, hd), F32), pltpu.VMEM((SUBLANES, ds), F32)],
        compiler_params=_cparams(("parallel", "arbitrary"), 48),
        name="rwkv_mix",
    )(p_all, zfirst, prm['mu_shift'], prm['w_lora'], prm['w0_decay'], prm['a0_iclr'], prm['k_k'], prm['k_a'],
      prm['lnx_w'], prm['lnx_b'], prm['r_k'], prm['seg_red'], prm['seg_exp'], s0)


def _bias_kernel(rb_ref, o_ref, tv_ref, *, tq, nk, width):
    h = pl.program_id(0)
    nrel = rb_ref.shape[1]
    m = lax.broadcasted_iota(jnp.int32, (nrel, width), 1)
    r = lax.broadcasted_iota(jnp.int32, (nrel, width), 0)
    m = jnp.where(m >= nk, m - width, m)
    sel = (jnp.clip(PREV_ROWS - m, -REL_CLIP, REL_CLIP) + REL_CLIP == r).astype(F32)
    tv_ref[...] = jnp.dot(rb_ref[...], sel, precision=lax.Precision.HIGHEST, preferred_element_type=F32)
    row = jnp.broadcast_to(tv_ref[pl.ds(h, 1), :], (tq, width))
    t = pltpu.roll(row, 0, axis=1, stride=1, stride_axis=0)[:, :nk]
    shift = CHUNK.bit_length() - 1
    col = lax.broadcasted_iota(jnp.int32, (tq, nk), 1)
    ci = lax.shift_right_logical(lax.broadcasted_iota(jnp.int32, (tq, nk), 0), shift)
    cj = lax.shift_right_logical(col, shift)
    band = jnp.where((cj >= ci) & (cj <= ci + N_PREV_CHUNKS), t, NEG_INF)
    for var in range(o_ref.shape[0]):
        o_ref[var] = jnp.where(col + var * tq - PREV_ROWS >= 0, band, NEG_INF)


def _bias_tile(rel_bias, *, tq):
    nh, nrel = rel_bias.shape
    nk = PREV_ROWS + tq
    n_var = PREV_ROWS // tq + 1
    width = -(-(nk + tq) // LANES) * LANES
    nrel_pad = -(-nrel // LANES) * LANES
    rb = jnp.pad(rel_bias, ((0, 0), (0, nrel_pad - nrel)))
    return pl.pallas_call(
        functools.partial(_bias_kernel, tq=tq, nk=nk, width=width),
        grid=(nh,),
        in_specs=[pl.BlockSpec((nh, nrel_pad), lambda h: (0, 0))],
        out_specs=pl.BlockSpec((None, n_var, tq, nk), lambda h: (h, 0, 0, 0)),
        out_shape=jax.ShapeDtypeStruct((nh, n_var, tq, nk), F32),
        scratch_shapes=[pltpu.VMEM((nh, width), F32)],
        compiler_params=_cparams(("arbitrary",), 32),
        name="band_bias",
    )(rb)


def _softmax_parts(s):
    p = jnp.exp(s - jnp.max(s, axis=-1, keepdims=True))
    return p.astype(BF16), jnp.sum(p, axis=-1, keepdims=True)


def _attn_prompt_kernel(*refs, nkb, dh):
    q_ref = refs[0]
    k_refs, v_refs = refs[1:1 + nkb], refs[1 + nkb:1 + 2 * nkb]
    bias_ref, o_ref = refs[1 + 2 * nkb:]
    for hh in range(ATT_HEADS_PER_STEP):
        cols = slice(hh * dh, (hh + 1) * dh)
        k = jnp.concatenate([kr[:, cols] for kr in k_refs], axis=0)
        v = jnp.concatenate([vr[:, cols] for vr in v_refs], axis=0)
        p, l = _softmax_parts(_dot_nt(q_ref[:, cols], k) + bias_ref[hh])
        o_ref[:, cols] = (jnp.dot(p, v, preferred_element_type=F32) / l).astype(o_ref.dtype)


def _attn_prompt(qkv, bias, *, nh, dh):
    t = qkv.shape[0]
    tq, hs = ATT_TQ, ATT_HEADS_PER_STEP
    assert tq % CHUNK == 0 and PREV_ROWS % tq == 0 and nh % hs == 0
    nkb = PREV_ROWS // tq + 1
    nhp = nh // hs

    def kv_spec(which, back):
        return pl.BlockSpec((tq, hs * dh), lambda h, qb: (jnp.maximum(qb - back, 0), which * nhp + h))

    return pl.pallas_call(
        functools.partial(_attn_prompt_kernel, nkb=nkb, dh=dh),
        grid=(nhp, t // tq),
        in_specs=[pl.BlockSpec((tq, hs * dh), lambda h, qb: (qb, h))]
                 + [kv_spec(1, nkb - 1 - j) for j in range(nkb)]
                 + [kv_spec(2, nkb - 1 - j) for j in range(nkb)]
                 + [pl.BlockSpec((hs, None, tq, PREV_ROWS + tq), lambda h, qb: (h, jnp.minimum(qb, nkb - 1), 0, 0))],
        out_specs=pl.BlockSpec((tq, hs * dh), lambda h, qb: (qb, h)),
        out_shape=jax.ShapeDtypeStruct((t, nh * dh), BF16),
        compiler_params=_cparams(("parallel", "parallel"), 32),
        name="band_attn_prompt",
    )(qkv, *([qkv] * (2 * nkb)), bias)


def _attn_sample_kernel(q_ref, kc_ref, kn_ref, vc_ref, vn_ref, bias_ref, o_ref, *, dh):
    for hh in range(ATT_HEADS_PER_STEP):
        cols = slice(hh * dh, (hh + 1) * dh)
        k = jnp.concatenate([kc_ref[:, :, cols], kn_ref[:, :, cols]], axis=1).astype(BF16)
        v = jnp.concatenate([vc_ref[:, :, cols], vn_ref[:, :, cols]], axis=1).astype(BF16)
        s = jnp.einsum('bqd,bkd->bqk', q_ref[:, :, cols].astype(BF16), k, preferred_element_type=F32)
        p, l = _softmax_parts(s + bias_ref[hh][None])
        o = jnp.einsum('bqk,bkd->bqd', p, v, preferred_element_type=F32) / l
        o_ref[:, :, cols] = o.astype(o_ref.dtype)


def _attn_sample(q, kc, kn, vc, vn, bias, *, nh, dh):
    b, t, d = q.shape
    nr = kc.shape[1]
    hs = ATT_HEADS_PER_STEP
    new = pl.BlockSpec((b, t, hs * dh), lambda h: (0, 0, h))
    old = pl.BlockSpec((b, nr, hs * dh), lambda h: (0, 0, h))
    return pl.pallas_call(
        functools.partial(_attn_sample_kernel, dh=dh),
        grid=(nh // hs,),
        in_specs=[new, old, new, old, new, pl.BlockSpec((hs, t, nr + t), lambda h: (h, 0, 0))],
        out_specs=new,
        out_shape=jax.ShapeDtypeStruct((b, t, d), BF16),
        compiler_params=_cparams(("parallel",), 48),
        name="band_attn_sample",
    )(q, kc, kn, vc, vn, bias)


def _run_group(x3, mods, pool_hist, shift_prev, wkv0, cache, pos0, p):
    bsz, t, d = x3.shape
    rows = bsz * t
    dp = pool_hist.shape[-1]
    ds = shift_prev.shape[-1]
    nh_att = p['rel_bias'].shape[0]
    dh = d // nh_att
    x = x3.reshape(rows, d)

    sh1, sc1, gt1, sh2, sc2, gt2 = mods[0]
    gn = p['g_norm'][0]
    proj = _norm_mm(x, gn[0:1], sc1, sh1, p['w_in'], tm=512, tn_cap=2304, ahead=True)
    assert t >= POOL_PAD - 1
    hist_pad = jnp.pad(pool_hist, ((0, 0), (POOL_PAD - pool_hist.shape[1], 0), (0, 0)))
    pool_out = _pool_mix(proj, ds, hist_pad, p['w_pool'], p['pool_scale'], bsz=bsz, t=t, pos0=pos0)
    zfirst = jnp.pad(shift_prev[:, None, :], ((0, 0), (SUBLANES - 1, 0), (0, 0)))
    yg, wkv_new = _rwkv_mix(proj, zfirst, wkv0, p, t=t)
    x = _mm_norm_res([pool_out, yg], p['w_out0'], gn[1:2], gt1, x)
    a = _norm_mm(x, gn[2:3], sc2, sh2, p['w_ff1'], layer=0, relu2=True, out_dtype=BF16, tn_cap=2048)
    x = _mm_norm_res([a], p['w_ff2'], gn[3:4], gt2, x, layer=0)
    tail = proj.reshape(bsz, t, -1)[:, t - (POOL_PAD - 1):]
    pool_new = tail[:, :, ds:]
    shift_new = tail[:, POOL_PAD - 2, :ds]

    sh1, sc1, gt1, sh2, sc2, gt2 = mods[1]
    gn = p['g_norm'][1]
    if cache is None:
        assert bsz == 1 and pos0 == 0
        qkv = _norm_mm(x, gn[0:1], sc1, sh1, p['w_qkv1'], out_dtype=BF16, tm=512, ahead=True)
        o = _attn_prompt(qkv, p['bias_tile'], nh=nh_att, dh=dh)
        keep = min(PREV_ROWS, t)
        kv = _norm_mm(x[t - keep:], gn[0:1], sc1, sh1, p['w_qkv1'], col0=d)
        k_new = kv[:, :d].reshape(1, keep, nh_att, dh)
        v_new = kv[:, d:].reshape(1, keep, nh_att, dh)
    else:
        cache_k, cache_v = cache
        nr = cache_k.shape[1]
        assert nr == PREV_ROWS and pos0 % CHUNK == 0 and pos0 >= nr and t <= CHUNK
        qkv3 = _norm_mm(x, gn[0:1], sc1, sh1, p['w_qkv1']).reshape(bsz, t, 3 * d)
        k_new, v_new = qkv3[:, :, d:2 * d], qkv3[:, :, 2 * d:]
        bias = p['bias_tile'][:, -1, :t, :nr + t]
        o = _attn_sample(qkv3[:, :, :d], cache_k.reshape(bsz, nr, d), k_new, cache_v.reshape(bsz, nr, d), v_new,
                         bias, nh=nh_att, dh=dh).reshape(rows, d)
        k_new = k_new.reshape(bsz, t, nh_att, dh)
        v_new = v_new.reshape(bsz, t, nh_att, dh)
    x = _mm_norm_res([o], p['w_out1'], gn[1:2], gt1, x)
    a = _norm_mm(x, gn[2:3], sc2, sh2, p['w_ff1'], layer=1, relu2=True, out_dtype=BF16, tn_cap=2048)
    x = _mm_norm_res([a], p['w_ff2'], gn[3:4], gt2, x, layer=1)
    return x.reshape(bsz, t, d), pool_new, shift_new, wkv_new, k_new, v_new


def kernel(x_prompt, x_sample, c_prompt, c_sample, state_l0_pool, state_l0_shift, state_l0_wkv, cache_l1_k, cache_l1_v, w_ada, b_ada, g_norm, w_in0, w_pool, pool_scale, mu_shift, w0_decay, w2_decay, a0_iclr, a2_iclr, g2_gate, k_k, k_a, r_k, lnx_w, lnx_b, w_out0, w_qkv1, rel_bias, w_out1, w_ff1, w_ff2):
    bp, tp, d = x_prompt.shape
    bs, ts, _ = x_sample.shape
    depth = w_ada.shape[0]
    dp = state_l0_pool.shape[-1]
    dr = d - dp
    nh = dr // RWKV_HEAD
    nh_att = rel_bias.shape[0]
    n_dec, n_iclr, n_gate = w2_decay.shape[0], a2_iclr.shape[0], g2_gate.shape[0]
    row = lambda a: a.reshape(1, -1)

    n_c = bp + bs
    c_all = jnp.pad(jnp.concatenate([c_prompt, c_sample], axis=0), ((0, -n_c % SUBLANES), (0, 0)))
    mod = _ada_mod(c_all, w_ada, b_ada)
    mods_p = [[mod[l, 0:bp, i * d:(i + 1) * d] for i in range(6)] for l in range(depth)]
    mods_s = [[jnp.repeat(mod[l, bp:n_c, i * d:(i + 1) * d], ts, axis=0) for i in range(6)] for l in range(depth)]

    seg = (lax.broadcasted_iota(jnp.int32, (dr, LANES), 0) // RWKV_HEAD
           == lax.broadcasted_iota(jnp.int32, (dr, LANES), 1)).astype(BF16)
    w_lora = jnp.zeros((3, n_dec + n_iclr + n_gate, dr), F32)
    w_lora = w_lora.at[0, :n_dec].set(w2_decay).at[1, n_dec:n_dec + n_iclr].set(a2_iclr)
    w_lora = w_lora.at[2, n_dec + n_iclr:].set(g2_gate)
    q_scale = jnp.concatenate([jnp.full((d,), (d // nh_att) ** -0.5, F32), jnp.ones((2 * d,), F32)])
    p = {
        'g_norm': g_norm,
        'w_in': jnp.concatenate([w_in0[:, dp:], w_in0[:, :dp]], axis=1).astype(BF16),
        'w_pool': w_pool.astype(BF16), 'pool_scale': row(pool_scale), 'mu_shift': row(mu_shift),
        'w_lora': w_lora.astype(BF16), 'w0_decay': row(w0_decay), 'a0_iclr': row(a0_iclr),
        'k_k': row(k_k), 'k_a': row(k_a), 'r_k': row(r_k), 'lnx_w': row(lnx_w), 'lnx_b': row(lnx_b),
        'seg_red': seg, 'seg_exp': seg.T,
        'w_out0': w_out0.astype(BF16), 'w_qkv1': (w_qkv1 * q_scale).astype(BF16), 'w_out1': w_out1.astype(BF16),
        'w_ff1': w_ff1.astype(BF16), 'w_ff2': w_ff2.astype(BF16),
        'rel_bias': rel_bias, 'bias_tile': _bias_tile(rel_bias, tq=ATT_TQ),
    }

    y_p, pool_p, shift_p, wkv_p, k_p, v_p = _run_group(
        x_prompt, mods_p, jnp.zeros((bp, POOL_PAD - 1, dp), F32), jnp.zeros((bp, w_in0.shape[1] - dp), F32),
        jnp.zeros((bp, nh, RWKV_HEAD, RWKV_HEAD), F32), None, 0, p)
    y_s, pool_s, shift_s, wkv_s, k_s, v_s = _run_group(
        x_sample, mods_s, state_l0_pool, state_l0_shift, state_l0_wkv, (cache_l1_k, cache_l1_v), PAST_LEN, p)
    return (y_p, y_s, pool_p, pool_s, shift_p, shift_s, wkv_p, wkv_s, k_p, v_p, k_s, v_s)
```

```python
import functools

import jax
import jax.numpy as jnp
from jax import lax
from jax.experimental import pallas as pl
from jax.experimental.pallas import tpu as pltpu

F32 = jnp.float32
BF16 = jnp.bfloat16

NORM_EPS = 1e-6
LNX_EPS = 64e-5
CHUNK = 64
POOL_WINDOWS = (2, 4, 8, 16)
POOL_PAD = 16
RWKV_HEAD = 64
DECAY_SCALE = 0.6065306597126334
WKV_CHUNK = 128
WKV_CHUNKS_PER_STEP = 1
N_PREV_CHUNKS = 8
PREV_ROWS = N_PREV_CHUNKS * CHUNK
REL_CLIP = 2 * CHUNK
PAST_LEN = 2048
NEG_INF = -1e30
V7X_VMEM_BYTES = 64 * 1024 * 1024
LANES = 128
SUBLANES = 8
ATT_TQ = 256
ATT_HEADS_PER_STEP = 4


def _cparams(semantics, vmem_mb):
    assert vmem_mb * 1024 * 1024 <= V7X_VMEM_BYTES
    return pltpu.CompilerParams(dimension_semantics=semantics, vmem_limit_bytes=vmem_mb * 1024 * 1024)


def _col_tile(n, cap):
    best = None
    for t in range(LANES, min(n, cap) + 1, LANES):
        if n % t == 0:
            best = t
    assert best is not None
    return best


def _dot(a, b):
    return jnp.dot(a.astype(BF16), b.astype(BF16), preferred_element_type=F32)


def _dot_nt(a, b):
    return lax.dot_general(a.astype(BF16), b.astype(BF16), (((1,), (1,)), ((), ())), preferred_element_type=F32)


def _split2(x):
    hi = x.astype(BF16)
    return hi, (x - hi.astype(F32)).astype(BF16)


def _split3(x):
    hi = x.astype(BF16)
    r1 = x - hi.astype(F32)
    mid = r1.astype(BF16)
    lo = (r1 - mid.astype(F32)).astype(BF16)
    return hi, mid, lo


def _segsum(x, red, exp):
    hi, lo = _split2(jnp.dot(x.astype(BF16), red, preferred_element_type=F32))
    return jnp.dot(hi, exp, preferred_element_type=F32) + jnp.dot(lo, exp, preferred_element_type=F32)


def _sigmoid(x):
    return 1.0 / (1.0 + jnp.exp(-x))


def _ada_kernel(c_ref, w_ref, b_ref, o_ref):
    c = c_ref[...]
    o_ref[...] = _dot(c * _sigmoid(c), w_ref[...]) + b_ref[...]


def _ada_mod(c_all, w_ada, b_ada, *, tn=1024):
    depth, d, n = w_ada.shape
    rows = c_all.shape[0]
    return pl.pallas_call(
        _ada_kernel,
        grid=(depth, n // tn),
        in_specs=[pl.BlockSpec((rows, d), lambda l, j: (0, 0)),
                  pl.BlockSpec((None, d, tn), lambda l, j: (l, 0, j)),
                  pl.BlockSpec((None, 1, tn), lambda l, j: (l, 0, j))],
        out_specs=pl.BlockSpec((None, rows, tn), lambda l, j: (l, 0, j)),
        out_shape=jax.ShapeDtypeStruct((depth, rows, n), F32),
        compiler_params=_cparams(("parallel", "parallel"), 40),
        name="ada_mod",
    )(c_all, w_ada, b_ada.reshape(depth, 1, n))


def _modulated_norm(x, g_ref, sc_ref, sh_ref):
    y = x * lax.rsqrt(jnp.mean(x * x, axis=-1, keepdims=True) + NORM_EPS) * g_ref[...]
    return (y * (1.0 + sc_ref[...]) + sh_ref[...]).astype(BF16)


def _mm_store(h_ref, w_ref, o_ref, relu2):
    acc = jnp.dot(h_ref[...], w_ref[...], preferred_element_type=F32)
    if relu2:
        acc = jnp.square(jnp.maximum(acc, 0.0))
    o_ref[...] = acc.astype(o_ref.dtype)


def _norm_mm_kernel(x_ref, g_ref, sc_ref, sh_ref, w_ref, o_ref, h_ref, *, relu2):
    @pl.when(pl.program_id(1) == 0)
    def _():
        h_ref[...] = _modulated_norm(x_ref[...], g_ref, sc_ref, sh_ref)

    _mm_store(h_ref, w_ref, o_ref, relu2)


def _norm_mm_ahead_kernel(x0_ref, xn_ref, g_ref, sc_ref, sh_ref, w_ref, o_ref, ha_ref, hb_ref, *, relu2, rps):
    i, j = pl.program_id(0), pl.program_id(1)

    @pl.when((i == 0) & (j == 0))
    def _():
        ha_ref[...] = _modulated_norm(x0_ref[...], g_ref, sc_ref, sh_ref)

    def step(cur_ref, nxt_ref):
        _mm_store(cur_ref, w_ref, o_ref, relu2)
        rows = pl.ds(pl.multiple_of(j * rps, rps), rps)
        nxt_ref[rows, :] = _modulated_norm(xn_ref[rows, :], g_ref, sc_ref, sh_ref)

    @pl.when(lax.rem(i, 2) == 0)
    def _():
        step(ha_ref, hb_ref)

    @pl.when(lax.rem(i, 2) == 1)
    def _():
        step(hb_ref, ha_ref)


def _norm_mm_ahead(x, g, sc, sh, w, *, layer, col0, relu2, out_dtype, tm=512, tn_cap=2048):
    rows, d = x.shape
    n = w.shape[-1] - col0
    ni = rows // tm
    tn = max(t for t in range(LANES, min(n, tn_cap) + 1, LANES)
             if n % t == 0 and tm % (n // t) == 0 and (tm // (n // t)) % (2 * SUBLANES) == 0)
    nj = n // tn
    assert col0 % tn == 0 and sc.shape[0] == 1
    cb = col0 // tn
    vec = pl.BlockSpec((1, d), lambda i, j: (0, 0))
    w_spec = (pl.BlockSpec((d, tn), lambda i, j: (0, cb + j)) if layer is None
              else pl.BlockSpec((None, d, tn), lambda i, j: (layer, 0, cb + j)))
    return pl.pallas_call(
        functools.partial(_norm_mm_ahead_kernel, relu2=relu2, rps=tm // nj),
        grid=(ni, nj),
        in_specs=[pl.BlockSpec((tm, d), lambda i, j: (0, 0), pipeline_mode=pl.Buffered(1)),
                  pl.BlockSpec((tm, d), lambda i, j: (jnp.minimum(i + 1, ni - 1), 0)),
                  vec, vec, vec, w_spec],
        out_specs=pl.BlockSpec((tm, tn), lambda i, j: (i, j)),
        out_shape=jax.ShapeDtypeStruct((rows, n), out_dtype),
        scratch_shapes=[pltpu.VMEM((tm, d), BF16), pltpu.VMEM((tm, d), BF16)],
        compiler_params=_cparams(("arbitrary", "arbitrary"), 56),
        name="norm_mm_ahead",
    )(x, x, g, sc, sh, w)


def _norm_mm(x, g, sc, sh, w, *, layer=None, col0=0, relu2=False, out_dtype=F32, tm=1024, tn_cap=1024,
             ahead=False):
    rows, d = x.shape
    if ahead and rows >= 2 * tm and sc.shape[0] == 1:
        return _norm_mm_ahead(x, g, sc, sh, w, layer=layer, col0=col0, relu2=relu2, out_dtype=out_dtype,
                              tn_cap=max(tn_cap, 2048))
    n = w.shape[-1] - col0
    tm = min(rows, tm)
    tn = _col_tile(n, tn_cap if rows > tm else 2 * tn_cap)
    assert col0 % tn == 0
    cb = col0 // tn
    mod_spec = (pl.BlockSpec((tm, d), lambda i, j: (i, 0)) if sc.shape[0] == rows
                else pl.BlockSpec((1, d), lambda i, j: (0, 0)))
    w_spec = (pl.BlockSpec((d, tn), lambda i, j: (0, cb + j)) if layer is None
              else pl.BlockSpec((None, d, tn), lambda i, j: (layer, 0, cb + j)))
    return pl.pallas_call(
        functools.partial(_norm_mm_kernel, relu2=relu2),
        grid=(rows // tm, n // tn),
        in_specs=[pl.BlockSpec((tm, d), lambda i, j: (i, 0)),
                  pl.BlockSpec((1, d), lambda i, j: (0, 0)),
                  mod_spec, mod_spec, w_spec],
        out_specs=pl.BlockSpec((tm, tn), lambda i, j: (i, j)),
        out_shape=jax.ShapeDtypeStruct((rows, n), out_dtype),
        scratch_shapes=[pltpu.VMEM((tm, d), BF16)],
        compiler_params=_cparams(("parallel", "arbitrary"), 56),
        name="norm_mm",
    )(x, g, sc, sh, w)


def _norm_res_epilogue(o, g_ref, gate_ref, x_ref, o_ref):
    y = o * lax.rsqrt(jnp.mean(o * o, axis=-1, keepdims=True) + NORM_EPS) * g_ref[...]
    o_ref[...] = x_ref[...] + gate_ref[...] * y


def _mm_norm_res_kernel(a_ref, w_ref, g_ref, gate_ref, x_ref, o_ref, acc_ref):
    k = pl.program_id(1)

    @pl.when(k == 0)
    def _():
        acc_ref[...] = jnp.zeros_like(acc_ref)

    acc_ref[...] += jnp.dot(a_ref[...], w_ref[...], preferred_element_type=F32)

    @pl.when(k == pl.num_programs(1) - 1)
    def _():
        _norm_res_epilogue(acc_ref[...], g_ref, gate_ref, x_ref, o_ref)


def _mm_norm_res_1step_kernel(*refs, n_a):
    a_refs = refs[:n_a]
    w_ref, g_ref, gate_ref, x_ref, o_ref = refs[n_a:]
    ka = a_refs[0].shape[1]
    o = jnp.dot(a_refs[0][...], w_ref[0:ka, :], preferred_element_type=F32)
    for idx in range(1, n_a):
        o = o + jnp.dot(a_refs[idx][...], w_ref[idx * ka:(idx + 1) * ka, :], preferred_element_type=F32)
    _norm_res_epilogue(o, g_ref, gate_ref, x_ref, o_ref)


def _mm_norm_res_lag_kernel(*refs, n_a, rps):
    a_refs = refs[:n_a]
    w_ref, g_ref, gate_ref, x_ref, o_ref, acc_a, acc_b = refs[n_a:]
    i, k = pl.program_id(0), pl.program_id(1)
    ka = a_refs[0].shape[1]

    @pl.when((i == 0) & (k == 0))
    def _():
        acc_a[...] = jnp.zeros_like(acc_a)
        acc_b[...] = jnp.zeros_like(acc_b)

    def step(cur_ref, prev_ref):
        part = jnp.dot(a_refs[0][...], w_ref[0:ka, :], preferred_element_type=F32)
        for idx in range(1, n_a):
            part = part + jnp.dot(a_refs[idx][...], w_ref[idx * ka:(idx + 1) * ka, :], preferred_element_type=F32)
        cur_ref[...] = jnp.where(k == 0, part, cur_ref[...] + part)
        rows = pl.ds(pl.multiple_of(k * rps, rps), rps)
        o = prev_ref[rows, :]
        y = o * lax.rsqrt(jnp.mean(o * o, axis=-1, keepdims=True) + NORM_EPS) * g_ref[...]
        o_ref[rows, :] = x_ref[rows, :] + gate_ref[...] * y

    @pl.when(lax.rem(i, 2) == 0)
    def _():
        step(acc_a, acc_b)

    @pl.when(lax.rem(i, 2) == 1)
    def _():
        step(acc_b, acc_a)


def _mm_norm_res_lag(a_list, w, g, gate, x, *, layer, tm, tk, nk):
    rows, d = x.shape
    ni = rows // tm
    n_a = len(a_list)
    ka = a_list[0].shape[1]
    assert gate.shape[0] == 1 and tm % nk == 0 and (tm // nk) % SUBLANES == 0
    lag_spec = pl.BlockSpec((tm, d), lambda i, k: (jnp.maximum(i - 1, 0), 0))
    vec = pl.BlockSpec((1, d), lambda i, k: (0, 0))
    w_spec = (pl.BlockSpec((tk, d), lambda i, k: (k, 0)) if layer is None
              else pl.BlockSpec((None, tk, d), lambda i, k: (layer, k, 0)))
    a_specs = [pl.BlockSpec((tm, ka if nk == 1 else tk), lambda i, k: (jnp.minimum(i, ni - 1), k)) for _ in a_list]
    return pl.pallas_call(
        functools.partial(_mm_norm_res_lag_kernel, n_a=n_a, rps=tm // nk),
        grid=(ni + 1, nk),
        in_specs=a_specs + [w_spec, vec, vec, lag_spec],
        out_specs=lag_spec,
        out_shape=jax.ShapeDtypeStruct((rows, d), F32),
        scratch_shapes=[pltpu.VMEM((tm, d), F32), pltpu.VMEM((tm, d), F32)],
        compiler_params=_cparams(("arbitrary", "arbitrary"), 56),
        name="mm_norm_res_lag",
    )(*a_list, w, g, gate, x)


def _mm_norm_res(a_list, w, g, gate, x, *, layer=None, tk_cap=2048):
    rows, d = x.shape
    tm = min(rows, 512)
    if rows >= 2 * tm and gate.shape[0] == 1 and len(a_list) == 1 and a_list[0].shape[1] > tk_cap:
        return _mm_norm_res_lag(a_list, w, g, gate, x, layer=layer, tm=tm, tk=tk_cap,
                                nk=a_list[0].shape[1] // tk_cap)
    ka = a_list[0].shape[1]
    n_a = len(a_list)
    k_all = n_a * ka
    tk = min(k_all, tk_cap)
    nk = k_all // tk
    assert nk == 1 or n_a == 1
    row_spec = pl.BlockSpec((tm, d), lambda i, k: (i, 0))
    gate_spec = row_spec if gate.shape[0] == rows else pl.BlockSpec((1, d), lambda i, k: (0, 0))
    w_spec = (pl.BlockSpec((tk, d), lambda i, k: (k, 0)) if layer is None
              else pl.BlockSpec((None, tk, d), lambda i, k: (layer, k, 0)))
    a_specs = [pl.BlockSpec((tm, ka if nk == 1 else tk), lambda i, k: (i, k)) for _ in a_list]
    return pl.pallas_call(
        functools.partial(_mm_norm_res_1step_kernel, n_a=n_a) if nk == 1 else _mm_norm_res_kernel,
        grid=(rows // tm, nk),
        in_specs=a_specs + [w_spec, pl.BlockSpec((1, d), lambda i, k: (0, 0)), gate_spec, row_spec],
        out_specs=row_spec,
        out_shape=jax.ShapeDtypeStruct((rows, d), F32),
        scratch_shapes=[] if nk == 1 else [pltpu.VMEM((tm, d), F32)],
        compiler_params=_cparams(("parallel", "arbitrary"), 56),
        name="mm_norm_res",
    )(*a_list, w, g, gate, x)


def _pool_kernel(*refs, tm, pos0):
    ng = len(POOL_WINDOWS)
    u_refs, prev_refs = refs[:ng], refs[ng:2 * ng]
    hist_ref, w_ref, ps_ref, o_ref = refs[2 * ng:]
    i = pl.program_id(1)
    gc = u_refs[0].shape[1]
    pos = (pos0 + i * tm + lax.broadcasted_iota(jnp.int32, (tm, gc), 0)).astype(F32)
    for gi, win in enumerate(POOL_WINDOWS):
        prev = jnp.where(i == 0, hist_ref[:, gi * gc:(gi + 1) * gc], prev_refs[gi][...])
        u = u_refs[gi][...]
        s, d = jnp.concatenate([prev, u], axis=0), 1
        while d < win:
            s = s + pltpu.roll(s, d, axis=0)
            d *= 2
        cnt = jnp.minimum(F32(win), pos + 1.0)
        dev = s[POOL_PAD:, :] / cnt - u
        o = _dot(dev, w_ref[gi]) * ps_ref[:, gi * gc:(gi + 1) * gc]
        o_ref[:, gi * gc:(gi + 1) * gc] = o.astype(o_ref.dtype)


def _pool_mix(p_all, col0, hist_pad, w_pool, pool_scale, *, bsz, t, pos0):
    dp = hist_pad.shape[-1]
    ng = len(POOL_WINDOWS)
    gc = dp // ng
    tm = min(t, 512)
    assert max(POOL_WINDOWS) <= POOL_PAD and tm % POOL_PAD == 0 and col0 % gc == 0
    nt, blk, cb = t // tm, tm // POOL_PAD, col0 // gc
    cur = [pl.BlockSpec((tm, gc), lambda b, i, gi=gi: (b * nt + i, cb + gi)) for gi in range(ng)]
    prev = [pl.BlockSpec((POOL_PAD, gc), lambda b, i, gi=gi: (b * nt * blk + jnp.maximum(i * blk - 1, 0), cb + gi))
            for gi in range(ng)]
    return pl.pallas_call(
        functools.partial(_pool_kernel, tm=tm, pos0=pos0),
        grid=(bsz, nt),
        in_specs=cur + prev + [pl.BlockSpec((None, POOL_PAD, dp), lambda b, i: (b, 0, 0)),
                               pl.BlockSpec(w_pool.shape, lambda b, i: (0, 0, 0)),
                               pl.BlockSpec((1, dp), lambda b, i: (0, 0))],
        out_specs=pl.BlockSpec((tm, dp), lambda b, i: (b * nt + i, 0)),
        out_shape=jax.ShapeDtypeStruct((bsz * t, dp), BF16),
        compiler_params=_cparams(("parallel", "parallel"), 32),
        name="pool_mix",
    )(*([p_all] * (2 * ng)), hist_pad, w_pool, pool_scale)


def _rwkv_token_prep(z, prev_row, mu_ref, wl_ref, w0_ref, a0_ref, kk_ref, ka_ref, red, exp, *, dr):
    rows = lax.broadcasted_iota(jnp.int32, z.shape, 0)
    z_prev = jnp.where(rows == 0, prev_row, pltpu.roll(z, 1, axis=0))
    zs = z + (z_prev - z) * mu_ref[...]
    r, k, v, xl = zs[:, :dr], zs[:, dr:2 * dr], zs[:, 2 * dr:3 * dr], zs[:, 3 * dr:]
    log_decay = -DECAY_SCALE * _sigmoid(w0_ref[...] + _dot(jnp.tanh(xl), wl_ref[0]))
    a = _sigmoid(a0_ref[...] + _dot(xl, wl_ref[1]))
    g = _dot(_sigmoid(xl), wl_ref[2])
    kk = k * kk_ref[...]
    kk = kk / jnp.maximum(jnp.sqrt(_segsum(kk * kk, red, exp)), 1e-12)
    return r, log_decay, k * (1.0 + (a - 1.0) * ka_ref[...]), v, kk, kk * a, g


def _wkv_chunks(r, lw, k, v, kk, b, s, *, nh, hd, n_tok):
    rows_all = lw.shape[0]
    n_sub = rows_all // n_tok
    sh_tok = n_tok.bit_length() - 1
    ri = lax.broadcasted_iota(jnp.int32, (rows_all, rows_all), 0)
    rj = lax.broadcasted_iota(jnp.int32, (rows_all, rows_all), 1)
    tri = ((ri >= rj) & (jnp.right_shift(ri, sh_tok) == jnp.right_shift(rj, sh_tok))).astype(BF16)
    hi, mid, lo = _split3(lw)
    cum = (jnp.dot(tri, hi, preferred_element_type=F32) + jnp.dot(tri, mid, preferred_element_type=F32)
           + jnp.dot(tri, lo, preferred_element_type=F32))
    tots = [cum[(c + 1) * n_tok - 1:(c + 1) * n_tok, :] for c in range(n_sub)]
    tot = jnp.concatenate([jnp.broadcast_to(tc, (n_tok, tc.shape[1])) for tc in tots], axis=0)
    e_neg = jnp.exp(-cum)
    e_rem = jnp.exp(tot - cum)
    ti = lax.broadcasted_iota(jnp.int32, (n_tok, n_tok), 0)
    tj = lax.broadcasted_iota(jnp.int32, (n_tok, n_tok), 1)

    def heads(x):
        return jnp.stack([x[c * n_tok:(c + 1) * n_tok, h * hd:(h + 1) * hd] for c in range(n_sub) for h in range(nh)])

    aw = heads(-kk * jnp.exp(cum - lw))
    rw = heads(r * jnp.exp(cum))
    bw = heads(b * e_neg)
    kw = heads(k * e_neg)
    bh = heads(b * e_rem)
    kh = heads(k * e_rem)
    v = heads(v)
    w_tot = jnp.stack([jnp.exp(tc)[:, h * hd:(h + 1) * hd] for tc in tots for h in range(nh)])

    def bmm(spec, x, y):
        return jnp.einsum(spec, x.astype(BF16), y.astype(BF16), preferred_element_type=F32)

    strict = (ti > tj)[None]
    incl = (ti >= tj)[None]
    aw_rw = jnp.concatenate([aw, rw], axis=1)
    on_b = bmm('hlk,hmk->hlm', aw_rw, bw)
    on_k = bmm('hlk,hmk->hlm', aw_rw, kw)
    a_ab = jnp.where(strict, on_b[:, :n_tok], 0.0)
    a_ak = jnp.where(strict, on_k[:, :n_tok], 0.0)
    a_rb_rk = jnp.concatenate([jnp.where(incl, on_b[:, n_tok:], 0.0), jnp.where(incl, on_k[:, n_tok:], 0.0)], axis=2)
    eye = jnp.where(ti == tj, 1.0, 0.0)[None]
    t_inv, blk = eye, 1
    while blk < n_tok:
        sh = blk.bit_length() - 1
        pair = ((jnp.right_shift(ti, sh + 1) == jnp.right_shift(tj, sh + 1))
                & ((jnp.right_shift(ti, sh) & 1) == 1) & ((jnp.right_shift(tj, sh) & 1) == 0))
        m = jnp.where(pair[None], a_ab, 0.0)
        if blk == 1:
            t_inv = t_inv + m
        else:
            t_inv = t_inv + bmm('hlm,hmv->hlv', t_inv, bmm('hlm,hmv->hlv', m, t_inv))
        blk *= 2
    rhs = jnp.concatenate([aw, bmm('hlm,hmv->hlv', a_ak, v)], axis=2)
    x = rhs + bmm('hlm,hmv->hlv', t_inv - eye, rhs)
    a_bar, u_c = x[:, :, :hd], x[:, :, hd:]

    abar_rw = jnp.concatenate([a_bar, rw], axis=1)
    bh_kh = jnp.concatenate([bh, kh], axis=1)
    ys = []
    for c in range(n_sub):
        sl = slice(c * nh, (c + 1) * nh)
        on_s = bmm('hlk,hvk->hlv', abar_rw[sl], s)
        uv = jnp.concatenate([on_s[:, :n_tok] + u_c[sl], v[sl]], axis=1)
        y = on_s[:, n_tok:] + bmm('hlm,hmv->hlv', a_rb_rk[sl], uv)
        s = s * w_tot[sl] + bmm('hlv,hlk->hvk', uv, bh_kh[sl])
        ys.append(jnp.concatenate([y[h] for h in range(nh)], axis=1))
    return jnp.concatenate(ys, axis=0), s


def _rwkv_kernel(z_ref, zfirst_ref, mu_ref, wl_ref, w0_ref, a0_ref, kk_ref, ka_ref, lnw_ref, lnb_ref, rk_ref,
                 red_ref, exp_ref, s0_ref, o_ref, sout_ref, s_ref, zlast_ref, *, nh, hd):
    c = pl.program_id(1)

    @pl.when(c == 0)
    def _():
        s_ref[...] = s0_ref[...]
        zlast_ref[...] = zfirst_ref[...]

    red, exp = red_ref[...], exp_ref[...]
    z = z_ref[...]
    n_tok = z.shape[0]
    r, lw, k, v, kk, b, g = _rwkv_token_prep(z, zlast_ref[SUBLANES - 1:, :], mu_ref, wl_ref, w0_ref, a0_ref,
                                             kk_ref, ka_ref, red, exp, dr=nh * hd)
    zlast_ref[...] = z[n_tok - SUBLANES:, :]
    y, s_new = _wkv_chunks(r, lw, k, v, kk, b, s_ref[...], nh=nh, hd=hd, n_tok=min(n_tok, WKV_CHUNK))
    s_ref[...] = s_new

    mu = _segsum(y, red, exp) * (1.0 / hd)
    yc = y - mu
    var = _segsum(yc * yc, red, exp) * (1.0 / hd)
    yn = yc * lax.rsqrt(var + LNX_EPS) * lnw_ref[...] + lnb_ref[...]
    bonus = _segsum(r * k * rk_ref[...], red, exp) * v
    o_ref[...] = ((yn + bonus) * g).astype(o_ref.dtype)

    @pl.when(c == pl.num_programs(1) - 1)
    def _():
        sout_ref[...] = s_new


def _rwkv_mix(p_all, zfirst, s0, prm, *, t):
    ds = zfirst.shape[-1]
    bsz, nh, hd = s0.shape[0], s0.shape[1], s0.shape[2]
    dr = nh * hd
    n_tok = min(t, WKV_CHUNK * WKV_CHUNKS_PER_STEP)
    nc = t // n_tok
    assert n_tok % SUBLANES == 0 and n_tok % min(n_tok, WKV_CHUNK) == 0
    vec = pl.BlockSpec((1, dr), lambda bi, c: (0, 0))
    tile = lambda w: pl.BlockSpec((n_tok, w), lambda bi, c: (bi * nc + c, 0))
    st = pl.BlockSpec((None, nh, hd, hd), lambda bi, c: (bi, 0, 0, 0))
    whole = lambda a: pl.BlockSpec(a.shape, lambda bi, c: (0,) * a.ndim)
    return pl.pallas_call(
        functools.partial(_rwkv_kernel, nh=nh, hd=hd),
        grid=(bsz, nc),
        in_specs=[tile(ds), pl.BlockSpec((None, SUBLANES, ds), lambda bi, c: (bi, 0, 0)),
                  pl.BlockSpec((1, ds), lambda bi, c: (0, 0)), whole(prm['w_lora'])] + [vec] * 7
                 + [whole(prm['seg_red']), whole(prm['seg_exp']), st],
        out_specs=[tile(dr), st],
        out_shape=[jax.ShapeDtypeStruct((bsz * t, dr), BF16), jax.ShapeDtypeStruct(s0.shape, F32)],
        scratch_shapes=[pltpu.VMEM((nh, hd, hd), F32), pltpu.VMEM((SUBLANES, ds), F32)],
        compiler_params=_cparams(("parallel", "arbitrary"), 48),
        name="rwkv_mix",
    )(p_all, zfirst, prm['mu_shift'], prm['w_lora'], prm['w0_decay'], prm['a0_iclr'], prm['k_k'], prm['k_a'],
      prm['lnx_w'], prm['lnx_b'], prm['r_k'], prm['seg_red'], prm['seg_exp'], s0)


def _bias_kernel(rb_ref, o_ref, tv_ref, *, tq, nk, width):
    h = pl.program_id(0)
    nrel = rb_ref.shape[1]
    m = lax.broadcasted_iota(jnp.int32, (nrel, width), 1)
    r = lax.broadcasted_iota(jnp.int32, (nrel, width), 0)
    m = jnp.where(m >= nk, m - width, m)
    sel = (jnp.clip(PREV_ROWS - m, -REL_CLIP, REL_CLIP) + REL_CLIP == r).astype(F32)
    tv_ref[...] = jnp.dot(rb_ref[...], sel, precision=lax.Precision.HIGHEST, preferred_element_type=F32)
    row = jnp.broadcast_to(tv_ref[pl.ds(h, 1), :], (tq, width))
    t = pltpu.roll(row, 0, axis=1, stride=1, stride_axis=0)[:, :nk]
    shift = CHUNK.bit_length() - 1
    col = lax.broadcasted_iota(jnp.int32, (tq, nk), 1)
    ci = lax.shift_right_logical(lax.broadcasted_iota(jnp.int32, (tq, nk), 0), shift)
    cj = lax.shift_right_logical(col, shift)
    band = jnp.where((cj >= ci) & (cj <= ci + N_PREV_CHUNKS), t, NEG_INF)
    for var in range(o_ref.shape[0]):
        o_ref[var] = jnp.where(col + var * tq - PREV_ROWS >= 0, band, NEG_INF)


def _bias_tile(rel_bias, *, tq):
    nh, nrel = rel_bias.shape
    nk = PREV_ROWS + tq
    n_var = PREV_ROWS // tq + 1
    width = -(-(nk + tq) // LANES) * LANES
    nrel_pad = -(-nrel // LANES) * LANES
    rb = jnp.pad(rel_bias, ((0, 0), (0, nrel_pad - nrel)))
    return pl.pallas_call(
        functools.partial(_bias_kernel, tq=tq, nk=nk, width=width),
        grid=(nh,),
        in_specs=[pl.BlockSpec((nh, nrel_pad), lambda h: (0, 0))],
        out_specs=pl.BlockSpec((None, n_var, tq, nk), lambda h: (h, 0, 0, 0)),
        out_shape=jax.ShapeDtypeStruct((nh, n_var, tq, nk), F32),
        scratch_shapes=[pltpu.VMEM((nh, width), F32)],
        compiler_params=_cparams(("arbitrary",), 32),
        name="band_bias",
    )(rb)


def _softmax_parts(s):
    p = jnp.exp(s - jnp.max(s, axis=-1, keepdims=True))
    return p.astype(BF16), jnp.sum(p, axis=-1, keepdims=True)


def _attn_prompt_kernel(*refs, nkb, dh):
    q_ref = refs[0]
    k_refs, v_refs = refs[1:1 + nkb], refs[1 + nkb:1 + 2 * nkb]
    bias_ref, o_ref = refs[1 + 2 * nkb:]
    cols = [slice(hh * dh, (hh + 1) * dh) for hh in range(ATT_HEADS_PER_STEP)]
    scores = [_dot_nt(q_ref[:, c], jnp.concatenate([kr[:, c] for kr in k_refs], axis=0)) + bias_ref[hh]
              for hh, c in enumerate(cols)]
    parts = [_softmax_parts(s) for s in scores]
    for c, (p, l) in zip(cols, parts):
        v = jnp.concatenate([vr[:, c] for vr in v_refs], axis=0)
        o_ref[:, c] = (jnp.dot(p, v, preferred_element_type=F32) / l).astype(o_ref.dtype)


def _attn_prompt(qkv, bias, *, nh, dh):
    t = qkv.shape[0]
    tq, hs = ATT_TQ, ATT_HEADS_PER_STEP
    assert tq % CHUNK == 0 and PREV_ROWS % tq == 0 and nh % hs == 0
    nkb = PREV_ROWS // tq + 1
    nhp = nh // hs

    def kv_spec(which, back):
        return pl.BlockSpec((tq, hs * dh), lambda h, qb: (jnp.maximum(qb - back, 0), which * nhp + h))

    return pl.pallas_call(
        functools.partial(_attn_prompt_kernel, nkb=nkb, dh=dh),
        grid=(nhp, t // tq),
        in_specs=[pl.BlockSpec((tq, hs * dh), lambda h, qb: (qb, h))]
                 + [kv_spec(1, nkb - 1 - j) for j in range(nkb)]
                 + [kv_spec(2, nkb - 1 - j) for j in range(nkb)]
                 + [pl.BlockSpec((hs, None, tq, PREV_ROWS + tq), lambda h, qb: (h, jnp.minimum(qb, nkb - 1), 0, 0))],
        out_specs=pl.BlockSpec((tq, hs * dh), lambda h, qb: (qb, h)),
        out_shape=jax.ShapeDtypeStruct((t, nh * dh), BF16),
        compiler_params=_cparams(("parallel", "parallel"), 48),
        name="band_attn_prompt",
    )(qkv, *([qkv] * (2 * nkb)), bias)


def _attn_sample_kernel(q_ref, kc_ref, kn_ref, vc_ref, vn_ref, bias_ref, o_ref, *, nh, dh, nr):
    for h in range(nh):
        cols = slice(h * dh, (h + 1) * dh)
        old = pl.ds(h, nr, stride=nh)
        k = jnp.concatenate([kc_ref[old, :], kn_ref[:, cols]], axis=0)
        v = jnp.concatenate([vc_ref[old, :], vn_ref[:, cols]], axis=0).astype(BF16)
        p, l = _softmax_parts(_dot_nt(q_ref[:, cols], k) + bias_ref[h])
        o_ref[:, cols] = (jnp.dot(p, v, preferred_element_type=F32) / l).astype(o_ref.dtype)


def _attn_sample(q, kc, kn, vc, vn, bias, *, nh, dh):
    b, t, d = q.shape
    nr = kc.shape[1]
    new = pl.BlockSpec((None, t, d), lambda bi: (bi, 0, 0))
    old = pl.BlockSpec((None, nr * nh, dh), lambda bi: (bi, 0, 0))
    return pl.pallas_call(
        functools.partial(_attn_sample_kernel, nh=nh, dh=dh, nr=nr),
        grid=(b,),
        in_specs=[new, old, new, old, new, pl.BlockSpec(bias.shape, lambda bi: (0, 0, 0))],
        out_specs=new,
        out_shape=jax.ShapeDtypeStruct((b, t, d), BF16),
        compiler_params=_cparams(("parallel",), 48),
        name="band_attn_sample",
    )(q, kc.reshape(b, nr * nh, dh), kn, vc.reshape(b, nr * nh, dh), vn, bias)


def _run_group(x3, mods, pool_hist, shift_prev, wkv0, cache, pos0, p):
    bsz, t, d = x3.shape
    rows = bsz * t
    dp = pool_hist.shape[-1]
    ds = shift_prev.shape[-1]
    nh_att = p['rel_bias'].shape[0]
    dh = d // nh_att
    x = x3.reshape(rows, d)

    sh1, sc1, gt1, sh2, sc2, gt2 = mods[0]
    gn = p['g_norm'][0]
    proj = _norm_mm(x, gn[0:1], sc1, sh1, p['w_in'], tm=512, tn_cap=2304, ahead=True)
    assert t >= POOL_PAD - 1
    hist_pad = jnp.pad(pool_hist, ((0, 0), (POOL_PAD - pool_hist.shape[1], 0), (0, 0)))
    pool_out = _pool_mix(proj, ds, hist_pad, p['w_pool'], p['pool_scale'], bsz=bsz, t=t, pos0=pos0)
    zfirst = jnp.pad(shift_prev[:, None, :], ((0, 0), (SUBLANES - 1, 0), (0, 0)))
    yg, wkv_new = _rwkv_mix(proj, zfirst, wkv0, p, t=t)
    x = _mm_norm_res([pool_out, yg], p['w_out0'], gn[1:2], gt1, x)
    a = _norm_mm(x, gn[2:3], sc2, sh2, p['w_ff1'], layer=0, relu2=True, out_dtype=BF16, tn_cap=2048)
    x = _mm_norm_res([a], p['w_ff2'], gn[3:4], gt2, x, layer=0)
    tail = proj.reshape(bsz, t, -1)[:, t - (POOL_PAD - 1):]
    pool_new = tail[:, :, ds:]
    shift_new = tail[:, POOL_PAD - 2, :ds]

    sh1, sc1, gt1, sh2, sc2, gt2 = mods[1]
    gn = p['g_norm'][1]
    if cache is None:
        assert bsz == 1 and pos0 == 0
        qkv = _norm_mm(x, gn[0:1], sc1, sh1, p['w_qkv1'], out_dtype=BF16, tm=512, ahead=True)
        o = _attn_prompt(qkv, p['bias_tile'], nh=nh_att, dh=dh)
        keep = min(PREV_ROWS, t)
        kv = _norm_mm(x[t - keep:], gn[0:1], sc1, sh1, p['w_qkv1'], col0=d)
        k_new = kv[:, :d].reshape(1, keep, nh_att, dh)
        v_new = kv[:, d:].reshape(1, keep, nh_att, dh)
    else:
        cache_k, cache_v = cache
        nr = cache_k.shape[1]
        assert nr == PREV_ROWS and pos0 % CHUNK == 0 and pos0 >= nr and t <= CHUNK
        qkv3 = _norm_mm(x, gn[0:1], sc1, sh1, p['w_qkv1']).reshape(bsz, t, 3 * d)
        k_new, v_new = qkv3[:, :, d:2 * d], qkv3[:, :, 2 * d:]
        bias = p['bias_tile'][:, -1, :t, :nr + t]
        o = _attn_sample(qkv3[:, :, :d], cache_k, k_new, cache_v, v_new, bias, nh=nh_att, dh=dh).reshape(rows, d)
        k_new = k_new.reshape(bsz, t, nh_att, dh)
        v_new = v_new.reshape(bsz, t, nh_att, dh)
    x = _mm_norm_res([o], p['w_out1'], gn[1:2], gt1, x)
    a = _norm_mm(x, gn[2:3], sc2, sh2, p['w_ff1'], layer=1, relu2=True, out_dtype=BF16, tn_cap=2048)
    x = _mm_norm_res([a], p['w_ff2'], gn[3:4], gt2, x, layer=1)
    return x.reshape(bsz, t, d), pool_new, shift_new, wkv_new, k_new, v_new


def kernel(x_prompt, x_sample, c_prompt, c_sample, state_l0_pool, state_l0_shift, state_l0_wkv, cache_l1_k, cache_l1_v, w_ada, b_ada, g_norm, w_in0, w_pool, pool_scale, mu_shift, w0_decay, w2_decay, a0_iclr, a2_iclr, g2_gate, k_k, k_a, r_k, lnx_w, lnx_b, w_out0, w_qkv1, rel_bias, w_out1, w_ff1, w_ff2):
    bp, tp, d = x_prompt.shape
    bs, ts, _ = x_sample.shape
    depth = w_ada.shape[0]
    dp = state_l0_pool.shape[-1]
    dr = d - dp
    nh = dr // RWKV_HEAD
    nh_att = rel_bias.shape[0]
    n_dec, n_iclr, n_gate = w2_decay.shape[0], a2_iclr.shape[0], g2_gate.shape[0]
    row = lambda a: a.reshape(1, -1)

    n_c = bp + bs
    c_all = jnp.pad(jnp.concatenate([c_prompt, c_sample], axis=0), ((0, -n_c % SUBLANES), (0, 0)))
    mod = _ada_mod(c_all, w_ada, b_ada)
    mods_p = [[mod[l, 0:bp, i * d:(i + 1) * d] for i in range(6)] for l in range(depth)]
    mods_s = [[jnp.repeat(mod[l, bp:n_c, i * d:(i + 1) * d], ts, axis=0) for i in range(6)] for l in range(depth)]

    seg = (lax.broadcasted_iota(jnp.int32, (dr, LANES), 0) // RWKV_HEAD
           == lax.broadcasted_iota(jnp.int32, (dr, LANES), 1)).astype(BF16)
    w_lora = jnp.zeros((3, n_dec + n_iclr + n_gate, dr), F32)
    w_lora = w_lora.at[0, :n_dec].set(w2_decay).at[1, n_dec:n_dec + n_iclr].set(a2_iclr)
    w_lora = w_lora.at[2, n_dec + n_iclr:].set(g2_gate)
    q_scale = jnp.concatenate([jnp.full((d,), (d // nh_att) ** -0.5, F32), jnp.ones((2 * d,), F32)])
    p = {
        'g_norm': g_norm,
        'w_in': jnp.concatenate([w_in0[:, dp:], w_in0[:, :dp]], axis=1).astype(BF16),
        'w_pool': w_pool.astype(BF16), 'pool_scale': row(pool_scale), 'mu_shift': row(mu_shift),
        'w_lora': w_lora.astype(BF16), 'w0_decay': row(w0_decay), 'a0_iclr': row(a0_iclr),
        'k_k': row(k_k), 'k_a': row(k_a), 'r_k': row(r_k), 'lnx_w': row(lnx_w), 'lnx_b': row(lnx_b),
        'seg_red': seg, 'seg_exp': seg.T,
        'w_out0': w_out0.astype(BF16), 'w_qkv1': (w_qkv1 * q_scale).astype(BF16), 'w_out1': w_out1.astype(BF16),
        'w_ff1': w_ff1.astype(BF16), 'w_ff2': w_ff2.astype(BF16),
        'rel_bias': rel_bias, 'bias_tile': _bias_tile(rel_bias, tq=ATT_TQ),
    }

    y_p, pool_p, shift_p, wkv_p, k_p, v_p = _run_group(
        x_prompt, mods_p, jnp.zeros((bp, POOL_PAD - 1, dp), F32), jnp.zeros((bp, w_in0.shape[1] - dp), F32),
        jnp.zeros((bp, nh, RWKV_HEAD, RWKV_HEAD), F32), None, 0, p)
    y_s, pool_s, shift_s, wkv_s, k_s, v_s = _run_group(
        x_sample, mods_s, state_l0_pool, state_l0_shift, state_l0_wkv, (cache_l1_k, cache_l1_v), PAST_LEN, p)
    return (y_p, y_s, pool_p, pool_s, shift_p, shift_s, wkv_p, wkv_s, k_p, v_p, k_s, v_s)
```

```python
import functools

import jax
import jax.numpy as jnp
from jax import lax
from jax.experimental import pallas as pl
from jax.experimental.pallas import tpu as pltpu

F32 = jnp.float32
BF16 = jnp.bfloat16

NORM_EPS = 1e-6
LNX_EPS = 64e-5
CHUNK = 64
POOL_WINDOWS = (2, 4, 8, 16)
POOL_PAD = 16
RWKV_HEAD = 64
DECAY_SCALE = 0.6065306597126334
WKV_CHUNK = 128
WKV_CHUNKS_PER_STEP = 1
N_PREV_CHUNKS = 8
PREV_ROWS = N_PREV_CHUNKS * CHUNK
REL_CLIP = 2 * CHUNK
PAST_LEN = 2048
NEG_INF = -1e30
V7X_VMEM_BYTES = 64 * 1024 * 1024
LANES = 128
SUBLANES = 8
ATT_TQ = 256
ATT_HEADS_PER_STEP = 4


def _cparams(semantics, vmem_mb):
    assert vmem_mb * 1024 * 1024 <= V7X_VMEM_BYTES
    return pltpu.CompilerParams(dimension_semantics=semantics, vmem_limit_bytes=vmem_mb * 1024 * 1024)


def _col_tile(n, cap):
    best = None
    for t in range(LANES, min(n, cap) + 1, LANES):
        if n % t == 0:
            best = t
    assert best is not None
    return best


def _dot(a, b):
    return jnp.dot(a.astype(BF16), b.astype(BF16), preferred_element_type=F32)


def _dot_nt(a, b):
    return lax.dot_general(a.astype(BF16), b.astype(BF16), (((1,), (1,)), ((), ())), preferred_element_type=F32)


def _split2(x):
    hi = x.astype(BF16)
    return hi, (x - hi.astype(F32)).astype(BF16)


def _split3(x):
    hi = x.astype(BF16)
    r1 = x - hi.astype(F32)
    mid = r1.astype(BF16)
    lo = (r1 - mid.astype(F32)).astype(BF16)
    return hi, mid, lo


def _segsum(x, red, exp):
    hi, lo = _split2(jnp.dot(x.astype(BF16), red, preferred_element_type=F32))
    return jnp.dot(hi, exp, preferred_element_type=F32) + jnp.dot(lo, exp, preferred_element_type=F32)


def _sigmoid(x):
    return 1.0 / (1.0 + jnp.exp(-x))


def _ada_kernel(c_ref, w_ref, b_ref, o_ref):
    c = c_ref[...]
    o_ref[...] = _dot(c * _sigmoid(c), w_ref[...]) + b_ref[...]


def _ada_mod(c_all, w_ada, b_ada, *, tn=1024):
    depth, d, n = w_ada.shape
    rows = c_all.shape[0]
    return pl.pallas_call(
        _ada_kernel,
        grid=(depth, n // tn),
        in_specs=[pl.BlockSpec((rows, d), lambda l, j: (0, 0)),
                  pl.BlockSpec((None, d, tn), lambda l, j: (l, 0, j)),
                  pl.BlockSpec((None, 1, tn), lambda l, j: (l, 0, j))],
        out_specs=pl.BlockSpec((None, rows, tn), lambda l, j: (l, 0, j)),
        out_shape=jax.ShapeDtypeStruct((depth, rows, n), F32),
        compiler_params=_cparams(("parallel", "parallel"), 40),
        name="ada_mod",
    )(c_all, w_ada, b_ada.reshape(depth, 1, n))


def _modulated_norm(x, g_ref, sc_ref, sh_ref):
    y = x * lax.rsqrt(jnp.mean(x * x, axis=-1, keepdims=True) + NORM_EPS) * g_ref[...]
    return (y * (1.0 + sc_ref[...]) + sh_ref[...]).astype(BF16)


def _mm_store(h_ref, w_ref, o_ref, relu2):
    acc = jnp.dot(h_ref[...], w_ref[...], preferred_element_type=F32)
    if relu2:
        acc = jnp.square(jnp.maximum(acc, 0.0))
    o_ref[...] = acc.astype(o_ref.dtype)


def _norm_mm_kernel(x_ref, g_ref, sc_ref, sh_ref, w_ref, o_ref, h_ref, *, relu2):
    @pl.when(pl.program_id(1) == 0)
    def _():
        h_ref[...] = _modulated_norm(x_ref[...], g_ref, sc_ref, sh_ref)

    _mm_store(h_ref, w_ref, o_ref, relu2)


def _norm_mm_ahead_kernel(x0_ref, xn_ref, g_ref, sc_ref, sh_ref, w_ref, o_ref, ha_ref, hb_ref, *, relu2, rps):
    i, j = pl.program_id(0), pl.program_id(1)

    @pl.when((i == 0) & (j == 0))
    def _():
        ha_ref[...] = _modulated_norm(x0_ref[...], g_ref, sc_ref, sh_ref)

    def step(cur_ref, nxt_ref):
        _mm_store(cur_ref, w_ref, o_ref, relu2)
        rows = pl.ds(pl.multiple_of(j * rps, rps), rps)
        nxt_ref[rows, :] = _modulated_norm(xn_ref[rows, :], g_ref, sc_ref, sh_ref)

    @pl.when(lax.rem(i, 2) == 0)
    def _():
        step(ha_ref, hb_ref)

    @pl.when(lax.rem(i, 2) == 1)
    def _():
        step(hb_ref, ha_ref)


def _norm_mm_ahead(x, g, sc, sh, w, *, layer, col0, relu2, out_dtype, tm=512, tn_cap=2048):
    rows, d = x.shape
    n = w.shape[-1] - col0
    ni = rows // tm
    tn = max(t for t in range(LANES, min(n, tn_cap) + 1, LANES)
             if n % t == 0 and tm % (n // t) == 0 and (tm // (n // t)) % (2 * SUBLANES) == 0)
    nj = n // tn
    assert col0 % tn == 0 and sc.shape[0] == 1
    cb = col0 // tn
    vec = pl.BlockSpec((1, d), lambda i, j: (0, 0))
    w_spec = (pl.BlockSpec((d, tn), lambda i, j: (0, cb + j)) if layer is None
              else pl.BlockSpec((None, d, tn), lambda i, j: (layer, 0, cb + j)))
    return pl.pallas_call(
        functools.partial(_norm_mm_ahead_kernel, relu2=relu2, rps=tm // nj),
        grid=(ni, nj),
        in_specs=[pl.BlockSpec((tm, d), lambda i, j: (0, 0), pipeline_mode=pl.Buffered(1)),
                  pl.BlockSpec((tm, d), lambda i, j: (jnp.minimum(i + 1, ni - 1), 0)),
                  vec, vec, vec, w_spec],
        out_specs=pl.BlockSpec((tm, tn), lambda i, j: (i, j)),
        out_shape=jax.ShapeDtypeStruct((rows, n), out_dtype),
        scratch_shapes=[pltpu.VMEM((tm, d), BF16), pltpu.VMEM((tm, d), BF16)],
        compiler_params=_cparams(("arbitrary", "arbitrary"), 56),
        name="norm_mm_ahead",
    )(x, x, g, sc, sh, w)


def _norm_mm(x, g, sc, sh, w, *, layer=None, col0=0, relu2=False, out_dtype=F32, tm=1024, tn_cap=1024,
             ahead=False):
    rows, d = x.shape
    if ahead and rows >= 2 * tm and sc.shape[0] == 1:
        return _norm_mm_ahead(x, g, sc, sh, w, layer=layer, col0=col0, relu2=relu2, out_dtype=out_dtype,
                              tn_cap=max(tn_cap, 2048))
    n = w.shape[-1] - col0
    tm = min(rows, tm)
    tn = _col_tile(n, tn_cap if rows > tm else 2 * tn_cap)
    assert col0 % tn == 0
    cb = col0 // tn
    mod_spec = (pl.BlockSpec((tm, d), lambda i, j: (i, 0)) if sc.shape[0] == rows
                else pl.BlockSpec((1, d), lambda i, j: (0, 0)))
    w_spec = (pl.BlockSpec((d, tn), lambda i, j: (0, cb + j)) if layer is None
              else pl.BlockSpec((None, d, tn), lambda i, j: (layer, 0, cb + j)))
    return pl.pallas_call(
        functools.partial(_norm_mm_kernel, relu2=relu2),
        grid=(rows // tm, n // tn),
        in_specs=[pl.BlockSpec((tm, d), lambda i, j: (i, 0)),
                  pl.BlockSpec((1, d), lambda i, j: (0, 0)),
                  mod_spec, mod_spec, w_spec],
        out_specs=pl.BlockSpec((tm, tn), lambda i, j: (i, j)),
        out_shape=jax.ShapeDtypeStruct((rows, n), out_dtype),
        scratch_shapes=[pltpu.VMEM((tm, d), BF16)],
        compiler_params=_cparams(("parallel", "arbitrary"), 56),
        name="norm_mm",
    )(x, g, sc, sh, w)


def _norm_res_epilogue(o, g_ref, gate_ref, x_ref, o_ref):
    y = o * lax.rsqrt(jnp.mean(o * o, axis=-1, keepdims=True) + NORM_EPS) * g_ref[...]
    o_ref[...] = x_ref[...] + gate_ref[...] * y


def _mm_norm_res_kernel(a_ref, w_ref, g_ref, gate_ref, x_ref, o_ref, acc_ref):
    k = pl.program_id(1)

    @pl.when(k == 0)
    def _():
        acc_ref[...] = jnp.zeros_like(acc_ref)

    acc_ref[...] += jnp.dot(a_ref[...], w_ref[...], preferred_element_type=F32)

    @pl.when(k == pl.num_programs(1) - 1)
    def _():
        _norm_res_epilogue(acc_ref[...], g_ref, gate_ref, x_ref, o_ref)


def _mm_norm_res_1step_kernel(*refs, n_a):
    a_refs = refs[:n_a]
    w_ref, g_ref, gate_ref, x_ref, o_ref = refs[n_a:]
    ka = a_refs[0].shape[1]
    o = jnp.dot(a_refs[0][...], w_ref[0:ka, :], preferred_element_type=F32)
    for idx in range(1, n_a):
        o = o + jnp.dot(a_refs[idx][...], w_ref[idx * ka:(idx + 1) * ka, :], preferred_element_type=F32)
    _norm_res_epilogue(o, g_ref, gate_ref, x_ref, o_ref)


def _mm_norm_res_lag_kernel(*refs, n_a, rps):
    a_refs = refs[:n_a]
    w_ref, g_ref, gate_ref, x_ref, o_ref, acc_a, acc_b = refs[n_a:]
    i, k = pl.program_id(0), pl.program_id(1)
    ka = a_refs[0].shape[1]

    @pl.when((i == 0) & (k == 0))
    def _():
        acc_a[...] = jnp.zeros_like(acc_a)
        acc_b[...] = jnp.zeros_like(acc_b)

    def step(cur_ref, prev_ref):
        part = jnp.dot(a_refs[0][...], w_ref[0:ka, :], preferred_element_type=F32)
        for idx in range(1, n_a):
            part = part + jnp.dot(a_refs[idx][...], w_ref[idx * ka:(idx + 1) * ka, :], preferred_element_type=F32)
        cur_ref[...] = jnp.where(k == 0, part, cur_ref[...] + part)
        rows = pl.ds(pl.multiple_of(k * rps, rps), rps)
        o = prev_ref[rows, :]
        y = o * lax.rsqrt(jnp.mean(o * o, axis=-1, keepdims=True) + NORM_EPS) * g_ref[...]
        o_ref[rows, :] = x_ref[rows, :] + gate_ref[...] * y

    @pl.when(lax.rem(i, 2) == 0)
    def _():
        step(acc_a, acc_b)

    @pl.when(lax.rem(i, 2) == 1)
    def _():
        step(acc_b, acc_a)


def _mm_norm_res_lag(a_list, w, g, gate, x, *, layer, tm, tk, nk):
    rows, d = x.shape
    ni = rows // tm
    n_a = len(a_list)
    ka = a_list[0].shape[1]
    assert gate.shape[0] == 1 and tm % nk == 0 and (tm // nk) % SUBLANES == 0
    lag_spec = pl.BlockSpec((tm, d), lambda i, k: (jnp.maximum(i - 1, 0), 0))
    vec = pl.BlockSpec((1, d), lambda i, k: (0, 0))
    w_spec = (pl.BlockSpec((tk, d), lambda i, k: (k, 0)) if layer is None
              else pl.BlockSpec((None, tk, d), lambda i, k: (layer, k, 0)))
    a_specs = [pl.BlockSpec((tm, ka if nk == 1 else tk), lambda i, k: (jnp.minimum(i, ni - 1), k)) for _ in a_list]
    return pl.pallas_call(
        functools.partial(_mm_norm_res_lag_kernel, n_a=n_a, rps=tm // nk),
        grid=(ni + 1, nk),
        in_specs=a_specs + [w_spec, vec, vec, lag_spec],
        out_specs=lag_spec,
        out_shape=jax.ShapeDtypeStruct((rows, d), F32),
        scratch_shapes=[pltpu.VMEM((tm, d), F32), pltpu.VMEM((tm, d), F32)],
        compiler_params=_cparams(("arbitrary", "arbitrary"), 56),
        name="mm_norm_res_lag",
    )(*a_list, w, g, gate, x)


def _mm_norm_res(a_list, w, g, gate, x, *, layer=None, tk_cap=2048):
    rows, d = x.shape
    tm = min(rows, 512)
    if rows >= 2 * tm and gate.shape[0] == 1 and len(a_list) == 1 and a_list[0].shape[1] > tk_cap:
        return _mm_norm_res_lag(a_list, w, g, gate, x, layer=layer, tm=tm, tk=tk_cap,
                                nk=a_list[0].shape[1] // tk_cap)
    ka = a_list[0].shape[1]
    n_a = len(a_list)
    k_all = n_a * ka
    tk = min(k_all, tk_cap)
    nk = k_all // tk
    assert nk == 1 or n_a == 1
    row_spec = pl.BlockSpec((tm, d), lambda i, k: (i, 0))
    gate_spec = row_spec if gate.shape[0] == rows else pl.BlockSpec((1, d), lambda i, k: (0, 0))
    w_spec = (pl.BlockSpec((tk, d), lambda i, k: (k, 0)) if layer is None
              else pl.BlockSpec((None, tk, d), lambda i, k: (layer, k, 0)))
    a_specs = [pl.BlockSpec((tm, ka if nk == 1 else tk), lambda i, k: (i, k)) for _ in a_list]
    return pl.pallas_call(
        functools.partial(_mm_norm_res_1step_kernel, n_a=n_a) if nk == 1 else _mm_norm_res_kernel,
        grid=(rows // tm, nk),
        in_specs=a_specs + [w_spec, pl.BlockSpec((1, d), lambda i, k: (0, 0)), gate_spec, row_spec],
        out_specs=row_spec,
        out_shape=jax.ShapeDtypeStruct((rows, d), F32),
        scratch_shapes=[] if nk == 1 else [pltpu.VMEM((tm, d), F32)],
        compiler_params=_cparams(("parallel", "arbitrary"), 56),
        name="mm_norm_res",
    )(*a_list, w, g, gate, x)


def _rwkv_token_prep(z, prev_row, mu_ref, wl_ref, w0_ref, a0_ref, kk_ref, ka_ref, red, exp, *, dr):
    rows = lax.broadcasted_iota(jnp.int32, z.shape, 0)
    z_prev = jnp.where(rows == 0, prev_row, pltpu.roll(z, 1, axis=0))
    zs = z + (z_prev - z) * mu_ref[...]
    r, k, v, xl = zs[:, :dr], zs[:, dr:2 * dr], zs[:, 2 * dr:3 * dr], zs[:, 3 * dr:]
    log_decay = -DECAY_SCALE * _sigmoid(w0_ref[...] + _dot(jnp.tanh(xl), wl_ref[0]))
    a = _sigmoid(a0_ref[...] + _dot(xl, wl_ref[1]))
    g = _dot(_sigmoid(xl), wl_ref[2])
    kk = k * kk_ref[...]
    kk = kk / jnp.maximum(jnp.sqrt(_segsum(kk * kk, red, exp)), 1e-12)
    return r, log_decay, k * (1.0 + (a - 1.0) * ka_ref[...]), v, kk, kk * a, g


def _wkv_chunks(r, lw, k, v, kk, b, s, *, nh, hd, n_tok):
    rows_all = lw.shape[0]
    n_sub = rows_all // n_tok
    sh_tok = n_tok.bit_length() - 1
    ri = lax.broadcasted_iota(jnp.int32, (rows_all, rows_all), 0)
    rj = lax.broadcasted_iota(jnp.int32, (rows_all, rows_all), 1)
    tri = ((ri >= rj) & (jnp.right_shift(ri, sh_tok) == jnp.right_shift(rj, sh_tok))).astype(BF16)
    hi, mid, lo = _split3(lw)
    cum = (jnp.dot(tri, hi, preferred_element_type=F32) + jnp.dot(tri, mid, preferred_element_type=F32)
           + jnp.dot(tri, lo, preferred_element_type=F32))
    tots = [cum[(c + 1) * n_tok - 1:(c + 1) * n_tok, :] for c in range(n_sub)]
    tot = jnp.concatenate([jnp.broadcast_to(tc, (n_tok, tc.shape[1])) for tc in tots], axis=0)
    e_neg = jnp.exp(-cum)
    e_rem = jnp.exp(tot - cum)
    ti = lax.broadcasted_iota(jnp.int32, (n_tok, n_tok), 0)
    tj = lax.broadcasted_iota(jnp.int32, (n_tok, n_tok), 1)

    def heads(x):
        return jnp.stack([x[c * n_tok:(c + 1) * n_tok, h * hd:(h + 1) * hd] for c in range(n_sub) for h in range(nh)])

    aw = heads(-kk * jnp.exp(cum - lw))
    rw = heads(r * jnp.exp(cum))
    bw = heads(b * e_neg)
    kw = heads(k * e_neg)
    bh = heads(b * e_rem)
    kh = heads(k * e_rem)
    v = heads(v)
    w_tot = jnp.stack([jnp.exp(tc)[:, h * hd:(h + 1) * hd] for tc in tots for h in range(nh)])

    def bmm(spec, x, y):
        return jnp.einsum(spec, x.astype(BF16), y.astype(BF16), preferred_element_type=F32)

    strict = (ti > tj)[None]
    incl = (ti >= tj)[None]
    aw_rw = jnp.concatenate([aw, rw], axis=1)
    on_b = bmm('hlk,hmk->hlm', aw_rw, bw)
    on_k = bmm('hlk,hmk->hlm', aw_rw, kw)
    a_ab = jnp.where(strict, on_b[:, :n_tok], 0.0)
    a_ak = jnp.where(strict, on_k[:, :n_tok], 0.0)
    a_rb_rk = jnp.concatenate([jnp.where(incl, on_b[:, n_tok:], 0.0), jnp.where(incl, on_k[:, n_tok:], 0.0)], axis=2)
    eye = jnp.where(ti == tj, 1.0, 0.0)[None]
    t_inv, blk = eye, 1
    while blk < n_tok:
        sh = blk.bit_length() - 1
        pair = ((jnp.right_shift(ti, sh + 1) == jnp.right_shift(tj, sh + 1))
                & ((jnp.right_shift(ti, sh) & 1) == 1) & ((jnp.right_shift(tj, sh) & 1) == 0))
        m = jnp.where(pair[None], a_ab, 0.0)
        if blk == 1:
            t_inv = t_inv + m
        else:
            t_inv = t_inv + bmm('hlm,hmv->hlv', t_inv, bmm('hlm,hmv->hlv', m, t_inv))
        blk *= 2
    rhs = jnp.concatenate([aw, bmm('hlm,hmv->hlv', a_ak, v)], axis=2)
    x = rhs + bmm('hlm,hmv->hlv', t_inv - eye, rhs)
    a_bar, u_c = x[:, :, :hd], x[:, :, hd:]

    abar_rw = jnp.concatenate([a_bar, rw], axis=1)
    bh_kh = jnp.concatenate([bh, kh], axis=1)
    ys = []
    for c in range(n_sub):
        sl = slice(c * nh, (c + 1) * nh)
        on_s = bmm('hlk,hvk->hlv', abar_rw[sl], s)
        uv = jnp.concatenate([on_s[:, :n_tok] + u_c[sl], v[sl]], axis=1)
        y = on_s[:, n_tok:] + bmm('hlm,hmv->hlv', a_rb_rk[sl], uv)
        s = s * w_tot[sl] + bmm('hlv,hlk->hvk', uv, bh_kh[sl])
        ys.append(jnp.concatenate([y[h] for h in range(nh)], axis=1))
    return jnp.concatenate(ys, axis=0), s


def _mixer0_kernel(*refs, nh, hd, pos0):
    ng = len(POOL_WINDOWS)
    u_refs = refs[:ng]
    (hist_ref, wp_ref, ps_ref, z_ref, zfirst_ref, mu_ref, wl_ref, w0_ref, a0_ref, kk_ref, ka_ref, lnw_ref, lnb_ref,
     rk_ref, red_ref, exp_ref, s0_ref, o_ref, sout_ref, s_ref, zlast_ref, ulast_ref) = refs[ng:]
    c = pl.program_id(1)

    @pl.when(c == 0)
    def _():
        s_ref[...] = s0_ref[...]
        zlast_ref[...] = zfirst_ref[...]
        ulast_ref[...] = hist_ref[...]

    n_tok = z_ref.shape[0]
    gc = u_refs[0].shape[1]
    dp = ng * gc

    pos = (pos0 + c * n_tok + lax.broadcasted_iota(jnp.int32, (n_tok, gc), 0)).astype(F32)
    for gi, win in enumerate(POOL_WINDOWS):
        cols = slice(gi * gc, (gi + 1) * gc)
        u = u_refs[gi][...]
        s, d = jnp.concatenate([ulast_ref[:, cols], u], axis=0), 1
        while d < win:
            s = s + pltpu.roll(s, d, axis=0)
            d *= 2
        dev = s[POOL_PAD:, :] / jnp.minimum(F32(win), pos + 1.0) - u
        o_ref[:, cols] = (_dot(dev, wp_ref[gi]) * ps_ref[:, cols]).astype(o_ref.dtype)
        ulast_ref[:, cols] = u[n_tok - POOL_PAD:, :]

    red, exp = red_ref[...], exp_ref[...]
    z = z_ref[...]
    r, lw, k, v, kk, b, g = _rwkv_token_prep(z, zlast_ref[SUBLANES - 1:, :], mu_ref, wl_ref, w0_ref, a0_ref,
                                             kk_ref, ka_ref, red, exp, dr=nh * hd)
    zlast_ref[...] = z[n_tok - SUBLANES:, :]
    y, s_new = _wkv_chunks(r, lw, k, v, kk, b, s_ref[...], nh=nh, hd=hd, n_tok=min(n_tok, WKV_CHUNK))
    s_ref[...] = s_new

    mu = _segsum(y, red, exp) * (1.0 / hd)
    yc = y - mu
    var = _segsum(yc * yc, red, exp) * (1.0 / hd)
    yn = yc * lax.rsqrt(var + LNX_EPS) * lnw_ref[...] + lnb_ref[...]
    bonus = _segsum(r * k * rk_ref[...], red, exp) * v
    o_ref[:, dp:] = ((yn + bonus) * g).astype(o_ref.dtype)

    @pl.when(c == pl.num_programs(1) - 1)
    def _():
        sout_ref[...] = s_new


def _mixer0(p_all, hist_pad, zfirst, s0, prm, *, t, pos0):
    ds, dp = zfirst.shape[-1], hist_pad.shape[-1]
    bsz, nh, hd = s0.shape[0], s0.shape[1], s0.shape[2]
    dr = nh * hd
    ng = len(POOL_WINDOWS)
    gc = dp // ng
    n_tok = min(t, WKV_CHUNK * WKV_CHUNKS_PER_STEP)
    nc = t // n_tok
    assert n_tok % SUBLANES == 0 and n_tok % min(n_tok, WKV_CHUNK) == 0
    assert max(POOL_WINDOWS) <= POOL_PAD <= n_tok and ds % gc == 0
    cb = ds // gc
    vec = pl.BlockSpec((1, dr), lambda bi, c: (0, 0))
    tile = lambda w: pl.BlockSpec((n_tok, w), lambda bi, c: (bi * nc + c, 0))
    st = pl.BlockSpec((None, nh, hd, hd), lambda bi, c: (bi, 0, 0, 0))
    whole = lambda a: pl.BlockSpec(a.shape, lambda bi, c: (0,) * a.ndim)
    u_specs = [pl.BlockSpec((n_tok, gc), lambda bi, c, gi=gi: (bi * nc + c, cb + gi)) for gi in range(ng)]
    return pl.pallas_call(
        functools.partial(_mixer0_kernel, nh=nh, hd=hd, pos0=pos0),
        grid=(bsz, nc),
        in_specs=u_specs + [pl.BlockSpec((None, POOL_PAD, dp), lambda bi, c: (bi, 0, 0)), whole(prm['w_pool']),
                            pl.BlockSpec((1, dp), lambda bi, c: (0, 0)),
                            tile(ds), pl.BlockSpec((None, SUBLANES, ds), lambda bi, c: (bi, 0, 0)),
                            pl.BlockSpec((1, ds), lambda bi, c: (0, 0)), whole(prm['w_lora'])] + [vec] * 7
                 + [whole(prm['seg_red']), whole(prm['seg_exp']), st],
        out_specs=[tile(dp + dr), st],
        out_shape=[jax.ShapeDtypeStruct((bsz * t, dp + dr), BF16), jax.ShapeDtypeStruct(s0.shape, F32)],
        scratch_shapes=[pltpu.VMEM((nh, hd, hd), F32), pltpu.VMEM((SUBLANES, ds), F32),
                        pltpu.VMEM((POOL_PAD, dp), F32)],
        compiler_params=_cparams(("parallel", "arbitrary"), 48),
        name="mixer0",
    )(*([p_all] * ng), hist_pad, prm['w_pool'], prm['pool_scale'], p_all, zfirst, prm['mu_shift'], prm['w_lora'],
      prm['w0_decay'], prm['a0_iclr'], prm['k_k'], prm['k_a'], prm['lnx_w'], prm['lnx_b'], prm['r_k'],
      prm['seg_red'], prm['seg_exp'], s0)


def _bias_kernel(rb_ref, o_ref, tv_ref, *, tq, nk, width):
    h = pl.program_id(0)
    nrel = rb_ref.shape[1]
    m = lax.broadcasted_iota(jnp.int32, (nrel, width), 1)
    r = lax.broadcasted_iota(jnp.int32, (nrel, width), 0)
    m = jnp.where(m >= nk, m - width, m)
    sel = (jnp.clip(PREV_ROWS - m, -REL_CLIP, REL_CLIP) + REL_CLIP == r).astype(F32)
    tv_ref[...] = jnp.dot(rb_ref[...], sel, precision=lax.Precision.HIGHEST, preferred_element_type=F32)
    row = jnp.broadcast_to(tv_ref[pl.ds(h, 1), :], (tq, width))
    t = pltpu.roll(row, 0, axis=1, stride=1, stride_axis=0)[:, :nk]
    shift = CHUNK.bit_length() - 1
    col = lax.broadcasted_iota(jnp.int32, (tq, nk), 1)
    ci = lax.shift_right_logical(lax.broadcasted_iota(jnp.int32, (tq, nk), 0), shift)
    cj = lax.shift_right_logical(col, shift)
    band = jnp.where((cj >= ci) & (cj <= ci + N_PREV_CHUNKS), t, NEG_INF)
    for var in range(o_ref.shape[0]):
        o_ref[var] = jnp.where(col + var * tq - PREV_ROWS >= 0, band, NEG_INF)


def _bias_tile(rel_bias, *, tq):
    nh, nrel = rel_bias.shape
    nk = PREV_ROWS + tq
    n_var = PREV_ROWS // tq + 1
    width = -(-(nk + tq) // LANES) * LANES
    nrel_pad = -(-nrel // LANES) * LANES
    rb = jnp.pad(rel_bias, ((0, 0), (0, nrel_pad - nrel)))
    return pl.pallas_call(
        functools.partial(_bias_kernel, tq=tq, nk=nk, width=width),
        grid=(nh,),
        in_specs=[pl.BlockSpec((nh, nrel_pad), lambda h: (0, 0))],
        out_specs=pl.BlockSpec((None, n_var, tq, nk), lambda h: (h, 0, 0, 0)),
        out_shape=jax.ShapeDtypeStruct((nh, n_var, tq, nk), F32),
        scratch_shapes=[pltpu.VMEM((nh, width), F32)],
        compiler_params=_cparams(("arbitrary",), 32),
        name="band_bias",
    )(rb)


def _softmax_parts(s):
    p = jnp.exp(s - jnp.max(s, axis=-1, keepdims=True))
    return p.astype(BF16), jnp.sum(p, axis=-1, keepdims=True)


def _attn_prompt_kernel(*refs, nkb, dh):
    q_ref = refs[0]
    k_refs, v_refs = refs[1:1 + nkb], refs[1 + nkb:1 + 2 * nkb]
    bias_ref, o_ref = refs[1 + 2 * nkb:]
    cols = [slice(hh * dh, (hh + 1) * dh) for hh in range(ATT_HEADS_PER_STEP)]
    scores = [_dot_nt(q_ref[:, c], jnp.concatenate([kr[:, c] for kr in k_refs], axis=0)) + bias_ref[hh]
              for hh, c in enumerate(cols)]
    parts = [_softmax_parts(s) for s in scores]
    for c, (p, l) in zip(cols, parts):
        v = jnp.concatenate([vr[:, c] for vr in v_refs], axis=0)
        o_ref[:, c] = (jnp.dot(p, v, preferred_element_type=F32) / l).astype(o_ref.dtype)


def _attn_prompt(qkv, bias, *, nh, dh):
    t = qkv.shape[0]
    tq, hs = ATT_TQ, ATT_HEADS_PER_STEP
    assert tq % CHUNK == 0 and PREV_ROWS % tq == 0 and nh % hs == 0
    nkb = PREV_ROWS // tq + 1
    nhp = nh // hs

    def kv_spec(which, back):
        return pl.BlockSpec((tq, hs * dh), lambda h, qb: (jnp.maximum(qb - back, 0), which * nhp + h))

    return pl.pallas_call(
        functools.partial(_attn_prompt_kernel, nkb=nkb, dh=dh),
        grid=(nhp, t // tq),
        in_specs=[pl.BlockSpec((tq, hs * dh), lambda h, qb: (qb, h))]
                 + [kv_spec(1, nkb - 1 - j) for j in range(nkb)]
                 + [kv_spec(2, nkb - 1 - j) for j in range(nkb)]
                 + [pl.BlockSpec((hs, None, tq, PREV_ROWS + tq), lambda h, qb: (h, jnp.minimum(qb, nkb - 1), 0, 0))],
        out_specs=pl.BlockSpec((tq, hs * dh), lambda h, qb: (qb, h)),
        out_shape=jax.ShapeDtypeStruct((t, nh * dh), BF16),
        compiler_params=_cparams(("parallel", "parallel"), 48),
        name="band_attn_prompt",
    )(qkv, *([qkv] * (2 * nkb)), bias)


def _attn_sample_kernel(q_ref, kc_ref, kn_ref, vc_ref, vn_ref, bias_ref, o_ref, *, nh, dh, nr):
    for h in range(nh):
        cols = slice(h * dh, (h + 1) * dh)
        old = pl.ds(h, nr, stride=nh)
        k = jnp.concatenate([kc_ref[old, :], kn_ref[:, cols]], axis=0)
        v = jnp.concatenate([vc_ref[old, :], vn_ref[:, cols]], axis=0).astype(BF16)
        p, l = _softmax_parts(_dot_nt(q_ref[:, cols], k) + bias_ref[h])
        o_ref[:, cols] = (jnp.dot(p, v, preferred_element_type=F32) / l).astype(o_ref.dtype)


def _attn_sample(q, kc, kn, vc, vn, bias, *, nh, dh):
    b, t, d = q.shape
    nr = kc.shape[1]
    new = pl.BlockSpec((None, t, d), lambda bi: (bi, 0, 0))
    old = pl.BlockSpec((None, nr * nh, dh), lambda bi: (bi, 0, 0))
    return pl.pallas_call(
        functools.partial(_attn_sample_kernel, nh=nh, dh=dh, nr=nr),
        grid=(b,),
        in_specs=[new, old, new, old, new, pl.BlockSpec(bias.shape, lambda bi: (0, 0, 0))],
        out_specs=new,
        out_shape=jax.ShapeDtypeStruct((b, t, d), BF16),
        compiler_params=_cparams(("parallel",), 48),
        name="band_attn_sample",
    )(q, kc.reshape(b, nr * nh, dh), kn, vc.reshape(b, nr * nh, dh), vn, bias)


def _run_group(x3, mods, pool_hist, shift_prev, wkv0, cache, pos0, p):
    bsz, t, d = x3.shape
    rows = bsz * t
    dp = pool_hist.shape[-1]
    ds = shift_prev.shape[-1]
    nh_att = p['rel_bias'].shape[0]
    dh = d // nh_att
    x = x3.reshape(rows, d)

    sh1, sc1, gt1, sh2, sc2, gt2 = mods[0]
    gn = p['g_norm'][0]
    proj = _norm_mm(x, gn[0:1], sc1, sh1, p['w_in'], tm=512, tn_cap=2304, ahead=True)
    assert t >= POOL_PAD - 1
    hist_pad = jnp.pad(pool_hist, ((0, 0), (POOL_PAD - pool_hist.shape[1], 0), (0, 0)))
    zfirst = jnp.pad(shift_prev[:, None, :], ((0, 0), (SUBLANES - 1, 0), (0, 0)))
    mixed, wkv_new = _mixer0(proj, hist_pad, zfirst, wkv0, p, t=t, pos0=pos0)
    x = _mm_norm_res([mixed], p['w_out0'], gn[1:2], gt1, x)
    a = _norm_mm(x, gn[2:3], sc2, sh2, p['w_ff1'], layer=0, relu2=True, out_dtype=BF16, tn_cap=2048)
    x = _mm_norm_res([a], p['w_ff2'], gn[3:4], gt2, x, layer=0)
    tail = proj.reshape(bsz, t, -1)[:, t - (POOL_PAD - 1):]
    pool_new = tail[:, :, ds:]
    shift_new = tail[:, POOL_PAD - 2, :ds]

    sh1, sc1, gt1, sh2, sc2, gt2 = mods[1]
    gn = p['g_norm'][1]
    if cache is None:
        assert bsz == 1 and pos0 == 0
        qkv = _norm_mm(x, gn[0:1], sc1, sh1, p['w_qkv1'], out_dtype=BF16, tm=512, ahead=True)
        o = _attn_prompt(qkv, p['bias_tile'], nh=nh_att, dh=dh)
        keep = min(PREV_ROWS, t)
        kv = _norm_mm(x[t - keep:], gn[0:1], sc1, sh1, p['w_qkv1'], col0=d)
        k_new = kv[:, :d].reshape(1, keep, nh_att, dh)
        v_new = kv[:, d:].reshape(1, keep, nh_att, dh)
    else:
        cache_k, cache_v = cache
        nr = cache_k.shape[1]
        assert nr == PREV_ROWS and pos0 % CHUNK == 0 and pos0 >= nr and t <= CHUNK
        qkv3 = _norm_mm(x, gn[0:1], sc1, sh1, p['w_qkv1']).reshape(bsz, t, 3 * d)
        k_new, v_new = qkv3[:, :, d:2 * d], qkv3[:, :, 2 * d:]
        bias = p['bias_tile'][:, -1, :t, :nr + t]
        o = _attn_sample(qkv3[:, :, :d], cache_k, k_new, cache_v, v_new, bias, nh=nh_att, dh=dh).reshape(rows, d)
        k_new = k_new.reshape(bsz, t, nh_att, dh)
        v_new = v_new.reshape(bsz, t, nh_att, dh)
    x = _mm_norm_res([o], p['w_out1'], gn[1:2], gt1, x)
    a = _norm_mm(x, gn[2:3], sc2, sh2, p['w_ff1'], layer=1, relu2=True, out_dtype=BF16, tn_cap=2048)
    x = _mm_norm_res([a], p['w_ff2'], gn[3:4], gt2, x, layer=1)
    return x.reshape(bsz, t, d), pool_new, shift_new, wkv_new, k_new, v_new


def kernel(x_prompt, x_sample, c_prompt, c_sample, state_l0_pool, state_l0_shift, state_l0_wkv, cache_l1_k, cache_l1_v, w_ada, b_ada, g_norm, w_in0, w_pool, pool_scale, mu_shift, w0_decay, w2_decay, a0_iclr, a2_iclr, g2_gate, k_k, k_a, r_k, lnx_w, lnx_b, w_out0, w_qkv1, rel_bias, w_out1, w_ff1, w_ff2):
    bp, tp, d = x_prompt.shape
    bs, ts, _ = x_sample.shape
    depth = w_ada.shape[0]
    dp = state_l0_pool.shape[-1]
    dr = d - dp
    nh = dr // RWKV_HEAD
    nh_att = rel_bias.shape[0]
    n_dec, n_iclr, n_gate = w2_decay.shape[0], a2_iclr.shape[0], g2_gate.shape[0]
    row = lambda a: a.reshape(1, -1)

    n_c = bp + bs
    c_all = jnp.pad(jnp.concatenate([c_prompt, c_sample], axis=0), ((0, -n_c % SUBLANES), (0, 0)))
    mod = _ada_mod(c_all, w_ada, b_ada)
    mods_p = [[mod[l, 0:bp, i * d:(i + 1) * d] for i in range(6)] for l in range(depth)]
    mods_s = [[jnp.repeat(mod[l, bp:n_c, i * d:(i + 1) * d], ts, axis=0) for i in range(6)] for l in range(depth)]

    seg = (lax.broadcasted_iota(jnp.int32, (dr, LANES), 0) // RWKV_HEAD
           == lax.broadcasted_iota(jnp.int32, (dr, LANES), 1)).astype(BF16)
    w_lora = jnp.zeros((3, n_dec + n_iclr + n_gate, dr), F32)
    w_lora = w_lora.at[0, :n_dec].set(w2_decay).at[1, n_dec:n_dec + n_iclr].set(a2_iclr)
    w_lora = w_lora.at[2, n_dec + n_iclr:].set(g2_gate)
    q_scale = jnp.concatenate([jnp.full((d,), (d // nh_att) ** -0.5, F32), jnp.ones((2 * d,), F32)])
    p = {
        'g_norm': g_norm,
        'w_in': jnp.concatenate([w_in0[:, dp:].astype(BF16), w_in0[:, :dp].astype(BF16)], axis=1),
        'w_pool': w_pool.astype(BF16), 'pool_scale': row(pool_scale), 'mu_shift': row(mu_shift),
        'w_lora': w_lora.astype(BF16), 'w0_decay': row(w0_decay), 'a0_iclr': row(a0_iclr),
        'k_k': row(k_k), 'k_a': row(k_a), 'r_k': row(r_k), 'lnx_w': row(lnx_w), 'lnx_b': row(lnx_b),
        'seg_red': seg, 'seg_exp': seg.T,
        'w_out0': w_out0.astype(BF16), 'w_qkv1': (w_qkv1 * q_scale).astype(BF16), 'w_out1': w_out1.astype(BF16),
        'w_ff1': w_ff1.astype(BF16), 'w_ff2': w_ff2.astype(BF16),
        'rel_bias': rel_bias, 'bias_tile': _bias_tile(rel_bias, tq=ATT_TQ),
    }

    y_p, pool_p, shift_p, wkv_p, k_p, v_p = _run_group(
        x_prompt, mods_p, jnp.zeros((bp, POOL_PAD - 1, dp), F32), jnp.zeros((bp, w_in0.shape[1] - dp), F32),
        jnp.zeros((bp, nh, RWKV_HEAD, RWKV_HEAD), F32), None, 0, p)
    y_s, pool_s, shift_s, wkv_s, k_s, v_s = _run_group(
        x_sample, mods_s, state_l0_pool, state_l0_shift, state_l0_wkv, (cache_l1_k, cache_l1_v), PAST_LEN, p)
    return (y_p, y_s, pool_p, pool_s, shift_p, shift_s, wkv_p, wkv_s, k_p, v_p, k_s, v_s)
```

```python
import functools

import jax
import jax.numpy as jnp
from jax import lax
from jax.experimental import pallas as pl
from jax.experimental.pallas import tpu as pltpu

F32 = jnp.float32
BF16 = jnp.bfloat16

NORM_EPS = 1e-6
LNX_EPS = 64e-5
CHUNK = 64
POOL_WINDOWS = (2, 4, 8, 16)
POOL_PAD = 16
RWKV_HEAD = 64
DECAY_SCALE = 0.6065306597126334
WKV_CHUNK = 128
WKV_CHUNKS_PER_STEP = 1
N_PREV_CHUNKS = 8
PREV_ROWS = N_PREV_CHUNKS * CHUNK
REL_CLIP = 2 * CHUNK
PAST_LEN = 2048
NEG_INF = -1e30
V7X_VMEM_BYTES = 64 * 1024 * 1024
LANES = 128
SUBLANES = 8
ATT_TQ = 256
ATT_HEADS_PER_STEP = 4


def _cparams(semantics, vmem_mb):
    assert vmem_mb * 1024 * 1024 <= V7X_VMEM_BYTES
    return pltpu.CompilerParams(dimension_semantics=semantics, vmem_limit_bytes=vmem_mb * 1024 * 1024)


def _col_tile(n, cap):
    best = None
    for t in range(LANES, min(n, cap) + 1, LANES):
        if n % t == 0:
            best = t
    assert best is not None
    return best


def _dot(a, b):
    return jnp.dot(a.astype(BF16), b.astype(BF16), preferred_element_type=F32)


def _dot_nt(a, b):
    return lax.dot_general(a.astype(BF16), b.astype(BF16), (((1,), (1,)), ((), ())), preferred_element_type=F32)


def _split2(x):
    hi = x.astype(BF16)
    return hi, (x - hi.astype(F32)).astype(BF16)


def _split3(x):
    hi = x.astype(BF16)
    r1 = x - hi.astype(F32)
    mid = r1.astype(BF16)
    lo = (r1 - mid.astype(F32)).astype(BF16)
    return hi, mid, lo


def _segsum(x, red, exp):
    hi, lo = _split2(jnp.dot(x.astype(BF16), red, preferred_element_type=F32))
    return jnp.dot(hi, exp, preferred_element_type=F32) + jnp.dot(lo, exp, preferred_element_type=F32)


def _sigmoid(x):
    return 1.0 / (1.0 + jnp.exp(-x))


def _ada_kernel(c_ref, w_ref, b_ref, o_ref):
    c = c_ref[...]
    o_ref[...] = _dot(c * _sigmoid(c), w_ref[...]) + b_ref[...]


def _ada_mod(c_all, w_ada, b_ada, *, tn=1024):
    depth, d, n = w_ada.shape
    rows = c_all.shape[0]
    return pl.pallas_call(
        _ada_kernel,
        grid=(depth, n // tn),
        in_specs=[pl.BlockSpec((rows, d), lambda l, j: (0, 0)),
                  pl.BlockSpec((None, d, tn), lambda l, j: (l, 0, j)),
                  pl.BlockSpec((None, 1, tn), lambda l, j: (l, 0, j))],
        out_specs=pl.BlockSpec((None, rows, tn), lambda l, j: (l, 0, j)),
        out_shape=jax.ShapeDtypeStruct((depth, rows, n), F32),
        compiler_params=_cparams(("parallel", "parallel"), 40),
        name="ada_mod",
    )(c_all, w_ada, b_ada.reshape(depth, 1, n))


def _modulated_norm(x, g_ref, sc_ref, sh_ref):
    y = x * lax.rsqrt(jnp.mean(x * x, axis=-1, keepdims=True) + NORM_EPS) * g_ref[...]
    return (y * (1.0 + sc_ref[...]) + sh_ref[...]).astype(BF16)


def _mm_store(h_ref, w_ref, o_ref, relu2):
    acc = jnp.dot(h_ref[...], w_ref[...], preferred_element_type=F32)
    if relu2:
        acc = jnp.square(jnp.maximum(acc, 0.0))
    o_ref[...] = acc.astype(o_ref.dtype)


def _norm_mm_kernel(x_ref, g_ref, sc_ref, sh_ref, w_ref, o_ref, h_ref, *, relu2):
    @pl.when(pl.program_id(1) == 0)
    def _():
        h_ref[...] = _modulated_norm(x_ref[...], g_ref, sc_ref, sh_ref)

    _mm_store(h_ref, w_ref, o_ref, relu2)


def _norm_mm_ahead_kernel(x0_ref, xn_ref, g_ref, sc_ref, sh_ref, w_ref, o_ref, ha_ref, hb_ref, *, relu2, rps):
    i, j = pl.program_id(0), pl.program_id(1)

    @pl.when((i == 0) & (j == 0))
    def _():
        ha_ref[...] = _modulated_norm(x0_ref[...], g_ref, sc_ref, sh_ref)

    def step(cur_ref, nxt_ref):
        _mm_store(cur_ref, w_ref, o_ref, relu2)
        rows = pl.ds(pl.multiple_of(j * rps, rps), rps)
        nxt_ref[rows, :] = _modulated_norm(xn_ref[rows, :], g_ref, sc_ref, sh_ref)

    @pl.when(lax.rem(i, 2) == 0)
    def _():
        step(ha_ref, hb_ref)

    @pl.when(lax.rem(i, 2) == 1)
    def _():
        step(hb_ref, ha_ref)


def _norm_mm_ahead(x, g, sc, sh, w, *, layer, col0, relu2, out_dtype, tm=512, tn_cap=2048):
    rows, d = x.shape
    n = w.shape[-1] - col0
    ni = rows // tm
    tn = max(t for t in range(LANES, min(n, tn_cap) + 1, LANES)
             if n % t == 0 and tm % (n // t) == 0 and (tm // (n // t)) % (2 * SUBLANES) == 0)
    nj = n // tn
    assert col0 % tn == 0 and sc.shape[0] == 1
    cb = col0 // tn
    vec = pl.BlockSpec((1, d), lambda i, j: (0, 0))
    w_spec = (pl.BlockSpec((d, tn), lambda i, j: (0, cb + j)) if layer is None
              else pl.BlockSpec((None, d, tn), lambda i, j: (layer, 0, cb + j)))
    return pl.pallas_call(
        functools.partial(_norm_mm_ahead_kernel, relu2=relu2, rps=tm // nj),
        grid=(ni, nj),
        in_specs=[pl.BlockSpec((tm, d), lambda i, j: (0, 0), pipeline_mode=pl.Buffered(1)),
                  pl.BlockSpec((tm, d), lambda i, j: (jnp.minimum(i + 1, ni - 1), 0)),
                  vec, vec, vec, w_spec],
        out_specs=pl.BlockSpec((tm, tn), lambda i, j: (i, j)),
        out_shape=jax.ShapeDtypeStruct((rows, n), out_dtype),
        scratch_shapes=[pltpu.VMEM((tm, d), BF16), pltpu.VMEM((tm, d), BF16)],
        compiler_params=_cparams(("arbitrary", "arbitrary"), 56),
        name="norm_mm_ahead",
    )(x, x, g, sc, sh, w)


def _norm_mm(x, g, sc, sh, w, *, layer=None, col0=0, relu2=False, out_dtype=F32, tm=1024, tn_cap=1024,
             ahead=False):
    rows, d = x.shape
    if ahead and rows >= 2 * tm and sc.shape[0] == 1:
        return _norm_mm_ahead(x, g, sc, sh, w, layer=layer, col0=col0, relu2=relu2, out_dtype=out_dtype,
                              tn_cap=max(tn_cap, 2048))
    n = w.shape[-1] - col0
    tm = min(rows, tm)
    tn = _col_tile(n, tn_cap if rows > tm else 2 * tn_cap)
    assert col0 % tn == 0
    cb = col0 // tn
    mod_spec = (pl.BlockSpec((tm, d), lambda i, j: (i, 0)) if sc.shape[0] == rows
                else pl.BlockSpec((1, d), lambda i, j: (0, 0)))
    w_spec = (pl.BlockSpec((d, tn), lambda i, j: (0, cb + j)) if layer is None
              else pl.BlockSpec((None, d, tn), lambda i, j: (layer, 0, cb + j)))
    return pl.pallas_call(
        functools.partial(_norm_mm_kernel, relu2=relu2),
        grid=(rows // tm, n // tn),
        in_specs=[pl.BlockSpec((tm, d), lambda i, j: (i, 0)),
                  pl.BlockSpec((1, d), lambda i, j: (0, 0)),
                  mod_spec, mod_spec, w_spec],
        out_specs=pl.BlockSpec((tm, tn), lambda i, j: (i, j)),
        out_shape=jax.ShapeDtypeStruct((rows, n), out_dtype),
        scratch_shapes=[pltpu.VMEM((tm, d), BF16)],
        compiler_params=_cparams(("parallel", "arbitrary"), 56),
        name="norm_mm",
    )(x, g, sc, sh, w)


def _norm_res_epilogue(o, g_ref, gate_ref, x_ref, o_ref):
    y = o * lax.rsqrt(jnp.mean(o * o, axis=-1, keepdims=True) + NORM_EPS) * g_ref[...]
    o_ref[...] = x_ref[...] + gate_ref[...] * y


def _mm_norm_res_kernel(a_ref, w_ref, g_ref, gate_ref, x_ref, o_ref, acc_ref):
    k = pl.program_id(1)

    @pl.when(k == 0)
    def _():
        acc_ref[...] = jnp.zeros_like(acc_ref)

    acc_ref[...] += jnp.dot(a_ref[...], w_ref[...], preferred_element_type=F32)

    @pl.when(k == pl.num_programs(1) - 1)
    def _():
        _norm_res_epilogue(acc_ref[...], g_ref, gate_ref, x_ref, o_ref)


def _mm_norm_res_1step_kernel(*refs, n_a):
    a_refs = refs[:n_a]
    w_ref, g_ref, gate_ref, x_ref, o_ref = refs[n_a:]
    ka = a_refs[0].shape[1]
    o = jnp.dot(a_refs[0][...], w_ref[0:ka, :], preferred_element_type=F32)
    for idx in range(1, n_a):
        o = o + jnp.dot(a_refs[idx][...], w_ref[idx * ka:(idx + 1) * ka, :], preferred_element_type=F32)
    _norm_res_epilogue(o, g_ref, gate_ref, x_ref, o_ref)


def _mm_norm_res_lag_kernel(*refs, n_a, rps):
    a_refs = refs[:n_a]
    w_ref, g_ref, gate_ref, x_ref, o_ref, acc_a, acc_b = refs[n_a:]
    i, k = pl.program_id(0), pl.program_id(1)
    ka = a_refs[0].shape[1]

    @pl.when((i == 0) & (k == 0))
    def _():
        acc_a[...] = jnp.zeros_like(acc_a)
        acc_b[...] = jnp.zeros_like(acc_b)

    def step(cur_ref, prev_ref):
        part = jnp.dot(a_refs[0][...], w_ref[0:ka, :], preferred_element_type=F32)
        for idx in range(1, n_a):
            part = part + jnp.dot(a_refs[idx][...], w_ref[idx * ka:(idx + 1) * ka, :], preferred_element_type=F32)
        cur_ref[...] = jnp.where(k == 0, part, cur_ref[...] + part)
        rows = pl.ds(pl.multiple_of(k * rps, rps), rps)
        o = prev_ref[rows, :]
        y = o * lax.rsqrt(jnp.mean(o * o, axis=-1, keepdims=True) + NORM_EPS) * g_ref[...]
        o_ref[rows, :] = x_ref[rows, :] + gate_ref[...] * y

    @pl.when(lax.rem(i, 2) == 0)
    def _():
        step(acc_a, acc_b)

    @pl.when(lax.rem(i, 2) == 1)
    def _():
        step(acc_b, acc_a)


def _mm_norm_res_lag(a_list, w, g, gate, x, *, layer, tm, tk, nk):
    rows, d = x.shape
    ni = rows // tm
    n_a = len(a_list)
    ka = a_list[0].shape[1]
    assert gate.shape[0] == 1 and tm % nk == 0 and (tm // nk) % SUBLANES == 0
    lag_spec = pl.BlockSpec((tm, d), lambda i, k: (jnp.maximum(i - 1, 0), 0))
    vec = pl.BlockSpec((1, d), lambda i, k: (0, 0))
    w_spec = (pl.BlockSpec((tk, d), lambda i, k: (k, 0)) if layer is None
              else pl.BlockSpec((None, tk, d), lambda i, k: (layer, k, 0)))
    a_specs = [pl.BlockSpec((tm, ka if nk == 1 else tk), lambda i, k: (jnp.minimum(i, ni - 1), k)) for _ in a_list]
    return pl.pallas_call(
        functools.partial(_mm_norm_res_lag_kernel, n_a=n_a, rps=tm // nk),
        grid=(ni + 1, nk),
        in_specs=a_specs + [w_spec, vec, vec, lag_spec],
        out_specs=lag_spec,
        out_shape=jax.ShapeDtypeStruct((rows, d), F32),
        scratch_shapes=[pltpu.VMEM((tm, d), F32), pltpu.VMEM((tm, d), F32)],
        compiler_params=_cparams(("arbitrary", "arbitrary"), 56),
        name="mm_norm_res_lag",
    )(*a_list, w, g, gate, x)


def _mm_norm_res(a_list, w, g, gate, x, *, layer=None, tk_cap=2048):
    rows, d = x.shape
    tm = min(rows, 512)
    if rows >= 2 * tm and gate.shape[0] == 1 and len(a_list) == 1 and a_list[0].shape[1] > tk_cap:
        return _mm_norm_res_lag(a_list, w, g, gate, x, layer=layer, tm=tm, tk=tk_cap,
                                nk=a_list[0].shape[1] // tk_cap)
    ka = a_list[0].shape[1]
    n_a = len(a_list)
    k_all = n_a * ka
    tk = min(k_all, tk_cap)
    nk = k_all // tk
    assert nk == 1 or n_a == 1
    row_spec = pl.BlockSpec((tm, d), lambda i, k: (i, 0))
    gate_spec = row_spec if gate.shape[0] == rows else pl.BlockSpec((1, d), lambda i, k: (0, 0))
    w_spec = (pl.BlockSpec((tk, d), lambda i, k: (k, 0)) if layer is None
              else pl.BlockSpec((None, tk, d), lambda i, k: (layer, k, 0)))
    a_specs = [pl.BlockSpec((tm, ka if nk == 1 else tk), lambda i, k: (i, k)) for _ in a_list]
    return pl.pallas_call(
        functools.partial(_mm_norm_res_1step_kernel, n_a=n_a) if nk == 1 else _mm_norm_res_kernel,
        grid=(rows // tm, nk),
        in_specs=a_specs + [w_spec, pl.BlockSpec((1, d), lambda i, k: (0, 0)), gate_spec, row_spec],
        out_specs=row_spec,
        out_shape=jax.ShapeDtypeStruct((rows, d), F32),
        scratch_shapes=[] if nk == 1 else [pltpu.VMEM((tm, d), F32)],
        compiler_params=_cparams(("parallel", "arbitrary"), 56),
        name="mm_norm_res",
    )(*a_list, w, g, gate, x)


def _rwkv_token_prep(z, prev_row, mu_ref, wl_ref, w0_ref, a0_ref, kk_ref, ka_ref, red, exp, *, dr):
    rows = lax.broadcasted_iota(jnp.int32, z.shape, 0)
    z_prev = jnp.where(rows == 0, prev_row, pltpu.roll(z, 1, axis=0))
    zs = z + (z_prev - z) * mu_ref[...]
    r, k, v, xl = zs[:, :dr], zs[:, dr:2 * dr], zs[:, 2 * dr:3 * dr], zs[:, 3 * dr:]
    log_decay = -DECAY_SCALE * _sigmoid(w0_ref[...] + _dot(jnp.tanh(xl), wl_ref[0]))
    a = _sigmoid(a0_ref[...] + _dot(xl, wl_ref[1]))
    g = _dot(_sigmoid(xl), wl_ref[2])
    kk = k * kk_ref[...]
    kk = kk / jnp.maximum(jnp.sqrt(_segsum(kk * kk, red, exp)), 1e-12)
    return r, log_decay, k * (1.0 + (a - 1.0) * ka_ref[...]), v, kk, kk * a, g


def _wkv_chunks(r, lw, k, v, kk, b, s, *, nh, hd, n_tok):
    rows_all = lw.shape[0]
    n_sub = rows_all // n_tok
    sh_tok = n_tok.bit_length() - 1
    ri = lax.broadcasted_iota(jnp.int32, (rows_all, rows_all), 0)
    rj = lax.broadcasted_iota(jnp.int32, (rows_all, rows_all), 1)
    tri = ((ri >= rj) & (jnp.right_shift(ri, sh_tok) == jnp.right_shift(rj, sh_tok))).astype(BF16)
    hi, mid, lo = _split3(lw)
    cum = (jnp.dot(tri, hi, preferred_element_type=F32) + jnp.dot(tri, mid, preferred_element_type=F32)
           + jnp.dot(tri, lo, preferred_element_type=F32))
    tots = [cum[(c + 1) * n_tok - 1:(c + 1) * n_tok, :] for c in range(n_sub)]
    tot = jnp.concatenate([jnp.broadcast_to(tc, (n_tok, tc.shape[1])) for tc in tots], axis=0)
    e_neg = jnp.exp(-cum)
    e_rem = jnp.exp(tot - cum)
    ti = lax.broadcasted_iota(jnp.int32, (n_tok, n_tok), 0)
    tj = lax.broadcasted_iota(jnp.int32, (n_tok, n_tok), 1)

    def heads(x):
        return jnp.stack([x[c * n_tok:(c + 1) * n_tok, h * hd:(h + 1) * hd] for c in range(n_sub) for h in range(nh)])

    aw = heads(-kk * jnp.exp(cum - lw))
    rw = heads(r * jnp.exp(cum))
    bw = heads(b * e_neg)
    kw = heads(k * e_neg)
    v = heads(v)
    w_tot = jnp.stack([jnp.exp(tc)[:, h * hd:(h + 1) * hd] for tc in tots for h in range(nh)])

    def bmm(spec, x, y):
        return jnp.einsum(spec, x.astype(BF16), y.astype(BF16), preferred_element_type=F32)

    strict = (ti > tj)[None]
    incl = (ti >= tj)[None]
    aw_rw = jnp.concatenate([aw, rw], axis=1)
    on_b = bmm('hlk,hmk->hlm', aw_rw, bw)
    on_k = bmm('hlk,hmk->hlm', aw_rw, kw)
    a_ab = jnp.where(strict, on_b[:, :n_tok], 0.0)
    a_ak = jnp.where(strict, on_k[:, :n_tok], 0.0)
    a_rb_rk = jnp.concatenate([jnp.where(incl, on_b[:, n_tok:], 0.0), jnp.where(incl, on_k[:, n_tok:], 0.0)], axis=2)
    rhs = jnp.concatenate([aw, bmm('hlm,hmv->hlv', a_ak, v)], axis=2)
    eye = jnp.where(ti == tj, 1.0, 0.0)[None]
    t_inv, blk = eye, 1
    while blk < n_tok:
        sh = blk.bit_length() - 1
        pair = ((jnp.right_shift(ti, sh + 1) == jnp.right_shift(tj, sh + 1))
                & ((jnp.right_shift(ti, sh) & 1) == 1) & ((jnp.right_shift(tj, sh) & 1) == 0))
        m = jnp.where(pair[None], a_ab, 0.0)
        if blk == 1:
            t_inv = t_inv + m
        else:
            t_inv = t_inv + bmm('hlm,hmv->hlv', t_inv, bmm('hlm,hmv->hlv', m, t_inv))
        blk *= 2
    x = rhs + bmm('hlm,hmv->hlv', t_inv - eye, rhs)
    a_bar, u_c = x[:, :, :hd], x[:, :, hd:]

    abar_rw = jnp.concatenate([a_bar, rw], axis=1)
    bh_kh = jnp.concatenate([heads(b * e_rem), heads(k * e_rem)], axis=1)
    ys = []
    for c in range(n_sub):
        sl = slice(c * nh, (c + 1) * nh)
        on_s = bmm('hlk,hvk->hlv', abar_rw[sl], s)
        uv = jnp.concatenate([on_s[:, :n_tok] + u_c[sl], v[sl]], axis=1)
        y = on_s[:, n_tok:] + bmm('hlm,hmv->hlv', a_rb_rk[sl], uv)
        s = s * w_tot[sl] + bmm('hlv,hlk->hvk', uv, bh_kh[sl])
        ys.append(jnp.concatenate([y[h] for h in range(nh)], axis=1))
    return jnp.concatenate(ys, axis=0), s


def _mixer0_kernel(*refs, nh, hd, pos0):
    ng = len(POOL_WINDOWS)
    u_refs = refs[:ng]
    (hist_ref, wp_ref, ps_ref, z_ref, zfirst_ref, mu_ref, wl_ref, w0_ref, a0_ref, kk_ref, ka_ref, lnw_ref, lnb_ref,
     rk_ref, red_ref, exp_ref, s0_ref, o_ref, sout_ref, s_ref, zlast_ref, ulast_ref) = refs[ng:]
    c = pl.program_id(1)

    @pl.when(c == 0)
    def _():
        s_ref[...] = s0_ref[...]
        zlast_ref[...] = zfirst_ref[...]
        ulast_ref[...] = hist_ref[...]

    n_tok = z_ref.shape[0]
    gc = u_refs[0].shape[1]
    dp = ng * gc

    pos = (pos0 + c * n_tok + lax.broadcasted_iota(jnp.int32, (n_tok, gc), 0)).astype(F32)
    for gi, win in enumerate(POOL_WINDOWS):
        cols = slice(gi * gc, (gi + 1) * gc)
        u = u_refs[gi][...]
        s, d = jnp.concatenate([ulast_ref[:, cols], u], axis=0), 1
        while d < win:
            s = s + pltpu.roll(s, d, axis=0)
            d *= 2
        dev = s[POOL_PAD:, :] / jnp.minimum(F32(win), pos + 1.0) - u
        o_ref[:, cols] = (_dot(dev, wp_ref[gi]) * ps_ref[:, cols]).astype(o_ref.dtype)
        ulast_ref[:, cols] = u[n_tok - POOL_PAD:, :]

    red, exp = red_ref[...], exp_ref[...]
    z = z_ref[...]
    r, lw, k, v, kk, b, g = _rwkv_token_prep(z, zlast_ref[SUBLANES - 1:, :], mu_ref, wl_ref, w0_ref, a0_ref,
                                             kk_ref, ka_ref, red, exp, dr=nh * hd)
    zlast_ref[...] = z[n_tok - SUBLANES:, :]
    bonus = _segsum(r * k * rk_ref[...], red, exp) * v
    y, s_new = _wkv_chunks(r, lw, k, v, kk, b, s_ref[...], nh=nh, hd=hd, n_tok=min(n_tok, WKV_CHUNK))
    s_ref[...] = s_new

    mu = _segsum(y, red, exp) * (1.0 / hd)
    yc = y - mu
    var = _segsum(yc * yc, red, exp) * (1.0 / hd)
    yn = yc * lax.rsqrt(var + LNX_EPS) * lnw_ref[...] + lnb_ref[...]
    o_ref[:, dp:] = ((yn + bonus) * g).astype(o_ref.dtype)

    @pl.when(c == pl.num_programs(1) - 1)
    def _():
        sout_ref[...] = s_new


def _mixer0(p_all, hist_pad, zfirst, s0, prm, *, t, pos0):
    ds, dp = zfirst.shape[-1], hist_pad.shape[-1]
    bsz, nh, hd = s0.shape[0], s0.shape[1], s0.shape[2]
    dr = nh * hd
    ng = len(POOL_WINDOWS)
    gc = dp // ng
    n_tok = min(t, WKV_CHUNK * WKV_CHUNKS_PER_STEP)
    nc = t // n_tok
    assert n_tok % SUBLANES == 0 and n_tok % min(n_tok, WKV_CHUNK) == 0
    assert max(POOL_WINDOWS) <= POOL_PAD <= n_tok and ds % gc == 0
    cb = ds // gc
    vec = pl.BlockSpec((1, dr), lambda bi, c: (0, 0))
    tile = lambda w: pl.BlockSpec((n_tok, w), lambda bi, c: (bi * nc + c, 0))
    st = pl.BlockSpec((None, nh, hd, hd), lambda bi, c: (bi, 0, 0, 0))
    whole = lambda a: pl.BlockSpec(a.shape, lambda bi, c: (0,) * a.ndim)
    u_specs = [pl.BlockSpec((n_tok, gc), lambda bi, c, gi=gi: (bi * nc + c, cb + gi)) for gi in range(ng)]
    return pl.pallas_call(
        functools.partial(_mixer0_kernel, nh=nh, hd=hd, pos0=pos0),
        grid=(bsz, nc),
        in_specs=u_specs + [pl.BlockSpec((None, POOL_PAD, dp), lambda bi, c: (bi, 0, 0)), whole(prm['w_pool']),
                            pl.BlockSpec((1, dp), lambda bi, c: (0, 0)),
                            tile(ds), pl.BlockSpec((None, SUBLANES, ds), lambda bi, c: (bi, 0, 0)),
                            pl.BlockSpec((1, ds), lambda bi, c: (0, 0)), whole(prm['w_lora'])] + [vec] * 7
                 + [whole(prm['seg_red']), whole(prm['seg_exp']), st],
        out_specs=[tile(dp + dr), st],
        out_shape=[jax.ShapeDtypeStruct((bsz * t, dp + dr), BF16), jax.ShapeDtypeStruct(s0.shape, F32)],
        scratch_shapes=[pltpu.VMEM((nh, hd, hd), F32), pltpu.VMEM((SUBLANES, ds), F32),
                        pltpu.VMEM((POOL_PAD, dp), F32)],
        compiler_params=_cparams(("parallel", "arbitrary"), 48),
        name="mixer0",
    )(*([p_all] * ng), hist_pad, prm['w_pool'], prm['pool_scale'], p_all, zfirst, prm['mu_shift'], prm['w_lora'],
      prm['w0_decay'], prm['a0_iclr'], prm['k_k'], prm['k_a'], prm['lnx_w'], prm['lnx_b'], prm['r_k'],
      prm['seg_red'], prm['seg_exp'], s0)


def _bias_kernel(rb_ref, o_ref, tv_ref, *, tq, nk, width):
    h = pl.program_id(0)
    nrel = rb_ref.shape[1]
    m = lax.broadcasted_iota(jnp.int32, (nrel, width), 1)
    r = lax.broadcasted_iota(jnp.int32, (nrel, width), 0)
    m = jnp.where(m >= nk, m - width, m)
    sel = (jnp.clip(PREV_ROWS - m, -REL_CLIP, REL_CLIP) + REL_CLIP == r).astype(F32)
    tv_ref[...] = jnp.dot(rb_ref[...], sel, precision=lax.Precision.HIGHEST, preferred_element_type=F32)
    row = jnp.broadcast_to(tv_ref[pl.ds(h, 1), :], (tq, width))
    t = pltpu.roll(row, 0, axis=1, stride=1, stride_axis=0)[:, :nk]
    shift = CHUNK.bit_length() - 1
    col = lax.broadcasted_iota(jnp.int32, (tq, nk), 1)
    ci = lax.shift_right_logical(lax.broadcasted_iota(jnp.int32, (tq, nk), 0), shift)
    cj = lax.shift_right_logical(col, shift)
    band = jnp.where((cj >= ci) & (cj <= ci + N_PREV_CHUNKS), t, NEG_INF)
    for var in range(o_ref.shape[0]):
        o_ref[var] = jnp.where(col + var * tq - PREV_ROWS >= 0, band, NEG_INF)


def _bias_tile(rel_bias, *, tq):
    nh, nrel = rel_bias.shape
    nk = PREV_ROWS + tq
    n_var = PREV_ROWS // tq + 1
    width = -(-(nk + tq) // LANES) * LANES
    nrel_pad = -(-nrel // LANES) * LANES
    rb = jnp.pad(rel_bias, ((0, 0), (0, nrel_pad - nrel)))
    return pl.pallas_call(
        functools.partial(_bias_kernel, tq=tq, nk=nk, width=width),
        grid=(nh,),
        in_specs=[pl.BlockSpec((nh, nrel_pad), lambda h: (0, 0))],
        out_specs=pl.BlockSpec((None, n_var, tq, nk), lambda h: (h, 0, 0, 0)),
        out_shape=jax.ShapeDtypeStruct((nh, n_var, tq, nk), F32),
        scratch_shapes=[pltpu.VMEM((nh, width), F32)],
        compiler_params=_cparams(("arbitrary",), 32),
        name="band_bias",
    )(rb)


def _softmax_parts(s):
    p = jnp.exp(s - jnp.max(s, axis=-1, keepdims=True))
    return p.astype(BF16), jnp.sum(p, axis=-1, keepdims=True)


def _attn_prompt_kernel(*refs, nkb, dh):
    q_ref = refs[0]
    k_refs, v_refs = refs[1:1 + nkb], refs[1 + nkb:1 + 2 * nkb]
    bias_ref, o_ref = refs[1 + 2 * nkb:]
    cols = [slice(hh * dh, (hh + 1) * dh) for hh in range(ATT_HEADS_PER_STEP)]
    scores = [_dot_nt(q_ref[:, c], jnp.concatenate([kr[:, c] for kr in k_refs], axis=0)) + bias_ref[hh]
              for hh, c in enumerate(cols)]
    parts = [_softmax_parts(s) for s in scores]
    for c, (p, l) in zip(cols, parts):
        v = jnp.concatenate([vr[:, c] for vr in v_refs], axis=0)
        o_ref[:, c] = (jnp.dot(p, v, preferred_element_type=F32) / l).astype(o_ref.dtype)


def _attn_prompt(qkv, bias, *, nh, dh):
    t = qkv.shape[0]
    tq, hs = ATT_TQ, ATT_HEADS_PER_STEP
    assert tq % CHUNK == 0 and PREV_ROWS % tq == 0 and nh % hs == 0
    nkb = PREV_ROWS // tq + 1
    nhp = nh // hs

    def kv_spec(which, back):
        return pl.BlockSpec((tq, hs * dh), lambda h, qb: (jnp.maximum(qb - back, 0), which * nhp + h))

    return pl.pallas_call(
        functools.partial(_attn_prompt_kernel, nkb=nkb, dh=dh),
        grid=(nhp, t // tq),
        in_specs=[pl.BlockSpec((tq, hs * dh), lambda h, qb: (qb, h))]
                 + [kv_spec(1, nkb - 1 - j) for j in range(nkb)]
                 + [kv_spec(2, nkb - 1 - j) for j in range(nkb)]
                 + [pl.BlockSpec((hs, None, tq, PREV_ROWS + tq), lambda h, qb: (h, jnp.minimum(qb, nkb - 1), 0, 0))],
        out_specs=pl.BlockSpec((tq, hs * dh), lambda h, qb: (qb, h)),
        out_shape=jax.ShapeDtypeStruct((t, nh * dh), BF16),
        compiler_params=_cparams(("parallel", "parallel"), 48),
        name="band_attn_prompt",
    )(qkv, *([qkv] * (2 * nkb)), bias)


def _attn_sample_kernel(q_ref, kc_ref, kn_ref, vc_ref, vn_ref, bias_ref, o_ref, *, nh, dh, nr):
    for h in range(nh):
        cols = slice(h * dh, (h + 1) * dh)
        old = pl.ds(h, nr, stride=nh)
        k = jnp.concatenate([kc_ref[old, :], kn_ref[:, cols]], axis=0)
        v = jnp.concatenate([vc_ref[old, :], vn_ref[:, cols]], axis=0).astype(BF16)
        p, l = _softmax_parts(_dot_nt(q_ref[:, cols], k) + bias_ref[h])
        o_ref[:, cols] = (jnp.dot(p, v, preferred_element_type=F32) / l).astype(o_ref.dtype)


def _attn_sample(q, kc, kn, vc, vn, bias, *, nh, dh):
    b, t, d = q.shape
    nr = kc.shape[1]
    new = pl.BlockSpec((None, t, d), lambda bi: (bi, 0, 0))
    old = pl.BlockSpec((None, nr * nh, dh), lambda bi: (bi, 0, 0))
    return pl.pallas_call(
        functools.partial(_attn_sample_kernel, nh=nh, dh=dh, nr=nr),
        grid=(b,),
        in_specs=[new, old, new, old, new, pl.BlockSpec(bias.shape, lambda bi: (0, 0, 0))],
        out_specs=new,
        out_shape=jax.ShapeDtypeStruct((b, t, d), BF16),
        compiler_params=_cparams(("parallel",), 48),
        name="band_attn_sample",
    )(q, kc.reshape(b, nr * nh, dh), kn, vc.reshape(b, nr * nh, dh), vn, bias)


def _run_group(x3, mods, pool_hist, shift_prev, wkv0, cache, pos0, p):
    bsz, t, d = x3.shape
    rows = bsz * t
    dp = pool_hist.shape[-1]
    ds = shift_prev.shape[-1]
    nh_att = p['rel_bias'].shape[0]
    dh = d // nh_att
    x = x3.reshape(rows, d)

    sh1, sc1, gt1, sh2, sc2, gt2 = mods[0]
    gn = p['g_norm'][0]
    proj = _norm_mm(x, gn[0:1], sc1, sh1, p['w_in'], tm=512, tn_cap=2304, ahead=True)
    assert t >= POOL_PAD - 1
    hist_pad = jnp.pad(pool_hist, ((0, 0), (POOL_PAD - pool_hist.shape[1], 0), (0, 0)))
    zfirst = jnp.pad(shift_prev[:, None, :], ((0, 0), (SUBLANES - 1, 0), (0, 0)))
    mixed, wkv_new = _mixer0(proj, hist_pad, zfirst, wkv0, p, t=t, pos0=pos0)
    x = _mm_norm_res([mixed], p['w_out0'], gn[1:2], gt1, x)
    a = _norm_mm(x, gn[2:3], sc2, sh2, p['w_ff1'], layer=0, relu2=True, out_dtype=BF16, tn_cap=2048)
    x = _mm_norm_res([a], p['w_ff2'], gn[3:4], gt2, x, layer=0)
    tail = proj.reshape(bsz, t, -1)[:, t - (POOL_PAD - 1):]
    pool_new = tail[:, :, ds:]
    shift_new = tail[:, POOL_PAD - 2, :ds]

    sh1, sc1, gt1, sh2, sc2, gt2 = mods[1]
    gn = p['g_norm'][1]
    if cache is None:
        assert bsz == 1 and pos0 == 0
        qkv = _norm_mm(x, gn[0:1], sc1, sh1, p['w_qkv1'], out_dtype=BF16, tm=512, ahead=True)
        o = _attn_prompt(qkv, p['bias_tile'], nh=nh_att, dh=dh)
        keep = min(PREV_ROWS, t)
        kv = _norm_mm(x[t - keep:], gn[0:1], sc1, sh1, p['w_qkv1'], col0=d)
        k_new = kv[:, :d].reshape(1, keep, nh_att, dh)
        v_new = kv[:, d:].reshape(1, keep, nh_att, dh)
    else:
        cache_k, cache_v = cache
        nr = cache_k.shape[1]
        assert nr == PREV_ROWS and pos0 % CHUNK == 0 and pos0 >= nr and t <= CHUNK
        qkv3 = _norm_mm(x, gn[0:1], sc1, sh1, p['w_qkv1']).reshape(bsz, t, 3 * d)
        k_new, v_new = qkv3[:, :, d:2 * d], qkv3[:, :, 2 * d:]
        bias = p['bias_tile'][:, -1, :t, :nr + t]
        o = _attn_sample(qkv3[:, :, :d], cache_k, k_new, cache_v, v_new, bias, nh=nh_att, dh=dh).reshape(rows, d)
        k_new = k_new.reshape(bsz, t, nh_att, dh)
        v_new = v_new.reshape(bsz, t, nh_att, dh)
    x = _mm_norm_res([o], p['w_out1'], gn[1:2], gt1, x)
    a = _norm_mm(x, gn[2:3], sc2, sh2, p['w_ff1'], layer=1, relu2=True, out_dtype=BF16, tn_cap=2048)
    x = _mm_norm_res([a], p['w_ff2'], gn[3:4], gt2, x, layer=1)
    return x.reshape(bsz, t, d), pool_new, shift_new, wkv_new, k_new, v_new


def kernel(x_prompt, x_sample, c_prompt, c_sample, state_l0_pool, state_l0_shift, state_l0_wkv, cache_l1_k, cache_l1_v, w_ada, b_ada, g_norm, w_in0, w_pool, pool_scale, mu_shift, w0_decay, w2_decay, a0_iclr, a2_iclr, g2_gate, k_k, k_a, r_k, lnx_w, lnx_b, w_out0, w_qkv1, rel_bias, w_out1, w_ff1, w_ff2):
    bp, tp, d = x_prompt.shape
    bs, ts, _ = x_sample.shape
    depth = w_ada.shape[0]
    dp = state_l0_pool.shape[-1]
    dr = d - dp
    nh = dr // RWKV_HEAD
    nh_att = rel_bias.shape[0]
    n_dec, n_iclr, n_gate = w2_decay.shape[0], a2_iclr.shape[0], g2_gate.shape[0]
    row = lambda a: a.reshape(1, -1)

    n_c = bp + bs
    c_all = jnp.pad(jnp.concatenate([c_prompt, c_sample], axis=0), ((0, -n_c % SUBLANES), (0, 0)))
    mod = _ada_mod(c_all, w_ada, b_ada)
    mods_p = [[mod[l, 0:bp, i * d:(i + 1) * d] for i in range(6)] for l in range(depth)]
    mods_s = [[jnp.repeat(mod[l, bp:n_c, i * d:(i + 1) * d], ts, axis=0) for i in range(6)] for l in range(depth)]

    seg = (lax.broadcasted_iota(jnp.int32, (dr, LANES), 0) // RWKV_HEAD
           == lax.broadcasted_iota(jnp.int32, (dr, LANES), 1)).astype(BF16)
    w_lora = jnp.zeros((3, n_dec + n_iclr + n_gate, dr), F32)
    w_lora = w_lora.at[0, :n_dec].set(w2_decay).at[1, n_dec:n_dec + n_iclr].set(a2_iclr)
    w_lora = w_lora.at[2, n_dec + n_iclr:].set(g2_gate)
    q_scale = jnp.concatenate([jnp.full((d,), (d // nh_att) ** -0.5, F32), jnp.ones((2 * d,), F32)])
    p = {
        'g_norm': g_norm,
        'w_in': jnp.concatenate([w_in0[:, dp:].astype(BF16), w_in0[:, :dp].astype(BF16)], axis=1),
        'w_pool': w_pool.astype(BF16), 'pool_scale': row(pool_scale), 'mu_shift': row(mu_shift),
        'w_lora': w_lora.astype(BF16), 'w0_decay': row(w0_decay), 'a0_iclr': row(a0_iclr),
        'k_k': row(k_k), 'k_a': row(k_a), 'r_k': row(r_k), 'lnx_w': row(lnx_w), 'lnx_b': row(lnx_b),
        'seg_red': seg, 'seg_exp': seg.T,
        'w_out0': w_out0.astype(BF16), 'w_qkv1': (w_qkv1 * q_scale).astype(BF16), 'w_out1': w_out1.astype(BF16),
        'w_ff1': w_ff1.astype(BF16), 'w_ff2': w_ff2.astype(BF16),
        'rel_bias': rel_bias, 'bias_tile': _bias_tile(rel_bias, tq=ATT_TQ),
    }

    y_p, pool_p, shift_p, wkv_p, k_p, v_p = _run_group(
        x_prompt, mods_p, jnp.zeros((bp, POOL_PAD - 1, dp), F32), jnp.zeros((bp, w_in0.shape[1] - dp), F32),
        jnp.zeros((bp, nh, RWKV_HEAD, RWKV_HEAD), F32), None, 0, p)
    y_s, pool_s, shift_s, wkv_s, k_s, v_s = _run_group(
        x_sample, mods_s, state_l0_pool, state_l0_shift, state_l0_wkv, (cache_l1_k, cache_l1_v), PAST_LEN, p)
    return (y_p, y_s, pool_p, pool_s, shift_p, shift_s, wkv_p, wkv_s, k_p, v_p, k_s, v_s)
```
